```python
import math
import jax, jax.numpy as jnp
from jax import lax
import numpy as np

D_MODEL = 1024
BATCH = 8
SEQ = 2048
DEPTH = 1
DEC_BATCH = 32
DEC_SEQ = 4
PAST_LEN = 16384
PAGE_SIZE = 128

HEAD_DIM = 64
N_ATT_HEADS = 12
ATT_WIDTH = N_ATT_HEADS * HEAD_DIM
CONV_GROUPS = 4
CONV_WIDTH = D_MODEL - ATT_WIDTH
CONV_K = 3
DILATION_PATTERNS = ((128, 1), (512, 4), (2048, 16))
WINDOW_MAX = max(w for w, _ in DILATION_PATTERNS)
Q_BLOCK = 128
ROPE_THETA = 10000.0
N_EXPERTS = 32
TOP_K = 4
D_FF = D_MODEL
SWIGLU_ALPHA = 1.702
SWIGLU_LIMIT = 7.0
MOE_BLOCK = 128
NORM_EPS = 1e-6
NEG_INF = -1e30
IN_PROJ_WIDTH = 3 * ATT_WIDTH + 3 * CONV_WIDTH
IN_SPLITS = (ATT_WIDTH, 2 * ATT_WIDTH, 3 * ATT_WIDTH,
             3 * ATT_WIDTH + CONV_WIDTH, 3 * ATT_WIDTH + 2 * CONV_WIDTH)

kernel_name = 'hybrid_dilated_attn_shortconv_moe_step'


def rmsnorm(x, g):
    xf = x.astype(jnp.float32)
    y = xf * lax.rsqrt(jnp.mean(xf * xf, axis=-1, keepdims=True) + NORM_EPS)
    return (y * g.astype(jnp.float32)).astype(x.dtype)


def rope(x, pos):
    half = HEAD_DIM // 2
    inv = ROPE_THETA ** (-jnp.arange(half, dtype=jnp.float32) / half)
    ang = pos.astype(jnp.float32)[:, None] * inv[None, :]
    cos = jnp.cos(ang)[None, :, None, :]
    sin = jnp.sin(ang)[None, :, None, :]
    xf = x.astype(jnp.float32)
    x1, x2 = xf[..., :half], xf[..., half:]
    return jnp.concatenate([x1 * cos - x2 * sin, x1 * sin + x2 * cos], axis=-1).astype(x.dtype)


def _combine(outs, lses):
    lse = jnp.stack(lses, 0)
    w = jax.nn.softmax(lse, axis=0)
    return jnp.einsum('p...h,p...he->...he', w, jnp.stack(outs, 0))


def dilated_attention_prompt(q, k, v):
    B, S = q.shape[0], q.shape[1]
    pad = jnp.zeros((B, WINDOW_MAX, N_ATT_HEADS, HEAD_DIM), k.dtype)
    k_pad = jnp.concatenate([pad, k], axis=1)
    v_pad = jnp.concatenate([pad, v], axis=1)
    n_blocks = S // Q_BLOCK

    def block(n):
        s0 = n * Q_BLOCK
        qb = lax.dynamic_slice_in_dim(q, s0, Q_BLOCK, axis=1).astype(jnp.float32)
        outs, lses = [], []
        for (w, d) in DILATION_PATTERNS:
            L = w + Q_BLOCK
            kb = lax.dynamic_slice_in_dim(k_pad, WINDOW_MAX + s0 - w, L, axis=1).astype(jnp.float32)
            vb = lax.dynamic_slice_in_dim(v_pad, WINDOW_MAX + s0 - w, L, axis=1).astype(jnp.float32)
            A, C = Q_BLOCK // d, L // d
            qr = qb.reshape(B, A, d, N_ATT_HEADS, HEAD_DIM)
            kr = kb.reshape(B, C, d, N_ATT_HEADS, HEAD_DIM)
            vr = vb.reshape(B, C, d, N_ATT_HEADS, HEAD_DIM)
            s = jnp.einsum('barhe,bcrhe->brhac', qr, kr)
            a_i = jnp.arange(A)[:, None]
            c_i = jnp.arange(C)[None, :]
            band = (c_i >= a_i) & (c_i <= a_i + w // d)
            key_pos = s0 - w + jnp.arange(C)[:, None] * d + jnp.arange(d)[None, :]
            mask = band[None, :, :] & (key_pos.T >= 0)[:, None, :]
            s = jnp.where(mask[None, :, None], s, NEG_INF)
            m = jnp.max(s, axis=-1, keepdims=True)
            p = jnp.exp(s - m)
            l = jnp.sum(p, axis=-1)
            l_t = jnp.transpose(l, (0, 3, 1, 2))
            o = jnp.einsum('brhac,bcrhe->barhe', p, vr) / l_t[..., None]
            lse = jnp.transpose(m[..., 0], (0, 3, 1, 2)) + jnp.log(l_t)
            outs.append(o.reshape(B, Q_BLOCK, N_ATT_HEADS, HEAD_DIM))
            lses.append(lse.reshape(B, Q_BLOCK, N_ATT_HEADS))
        return _combine(outs, lses)

    ob = lax.map(block, jnp.arange(n_blocks))
    return jnp.transpose(ob, (1, 0, 2, 3, 4)).reshape(B, S, N_ATT_HEADS, HEAD_DIM)


def dilated_attention_sample(q, k_all, v_all, n_past):
    T = q.shape[1]
    qf = q.astype(jnp.float32)
    outs, lses = [], []
    for (w, d) in DILATION_PATTERNS:
        offs = jnp.arange(w // d + 1) * d
        idx = n_past + jnp.arange(T)[:, None] - offs[None, :]
        valid = idx >= 0
        idx = jnp.maximum(idx, 0)
        kg = k_all[:, idx].astype(jnp.float32)
        vg = v_all[:, idx].astype(jnp.float32)
        s = jnp.einsum('bthe,btkhe->bthk', qf, kg)
        s = jnp.where(valid[None, :, None, :], s, NEG_INF)
        m = jnp.max(s, axis=-1, keepdims=True)
        p = jnp.exp(s - m)
        l = jnp.sum(p, axis=-1)
        o = jnp.einsum('bthk,btkhe->bthe', p, vg) / l[..., None]
        outs.append(o)
        lses.append(m[..., 0] + jnp.log(l))
    return _combine(outs, lses)


def _project(h, w_in):
    z = jnp.einsum('bsd,df->bsf', h, w_in)
    B, S = h.shape[0], h.shape[1]
    q, k, v, gb, gc, u = jnp.split(z, IN_SPLITS, axis=-1)
    shp = (B, S, N_ATT_HEADS, HEAD_DIM)
    return q.reshape(shp), k.reshape(shp), v.reshape(shp), gb, gc, u


def _short_conv(u_ext, conv_w):
    rhs = conv_w.astype(u_ext.dtype)[:, None, :]
    return lax.conv_general_dilated(u_ext, rhs, window_strides=(1,), padding='VALID',
                                    dimension_numbers=('NWC', 'WIO', 'NWC'),
                                    feature_group_count=CONV_WIDTH)


def _merge(att, conv_out, g_attn_out, g_conv_out, w_out):
    B, S = att.shape[0], att.shape[1]
    att = att.reshape(B, S, ATT_WIDTH)
    cat = jnp.concatenate([rmsnorm(att, g_attn_out), rmsnorm(conv_out, g_conv_out)], axis=-1)
    return jnp.einsum('bsf,fd->bsd', cat, w_out)


def _prompt_mixer(h, w_in, conv_w, g_attn_out, g_conv_out, w_out):
    S = h.shape[1]
    q, k, v, gb, gc, u = _project(h, w_in)
    pos = jnp.arange(S)
    q = rope(q, pos) * (HEAD_DIM ** -0.5)
    k = rope(k, pos)
    att = dilated_attention_prompt(q, k, v).astype(h.dtype)
    cu = gc * u
    cu_ext = jnp.pad(cu, ((0, 0), (CONV_K - 1, 0), (0, 0)))
    conv_out = gb * _short_conv(cu_ext, conv_w)
    y = _merge(att, conv_out, g_attn_out, g_conv_out, w_out)
    keep = min(WINDOW_MAX, S)
    return y, k[:, S - keep:], v[:, S - keep:], cu[:, S - (CONV_K - 1):]


def _sample_mixer(h, k_buf, v_buf, conv_buf, w_in, conv_w, g_attn_out, g_conv_out, w_out):
    T = h.shape[1]
    n_past = k_buf.shape[1]
    q, k, v, gb, gc, u = _project(h, w_in)
    pos = PAST_LEN + jnp.arange(T)
    q = rope(q, pos) * (HEAD_DIM ** -0.5)
    k = rope(k, pos)
    k_all = jnp.concatenate([k_buf.astype(k.dtype), k], axis=1)
    v_all = jnp.concatenate([v_buf.astype(v.dtype), v], axis=1)
    att = dilated_attention_sample(q, k_all, v_all, n_past).astype(h.dtype)
    cu = gc * u
    cu_ext = jnp.concatenate([conv_buf.astype(cu.dtype), cu], axis=1)
    conv_out = gb * _short_conv(cu_ext, conv_w)
    y = _merge(att, conv_out, g_attn_out, g_conv_out, w_out)
    return y, k, v, cu_ext[:, -(CONV_K - 1):]


def _moe(h, w_router, b_router, w_gate, b_gate, w_up, b_up, w_down, b_down):
    B, S, D = h.shape
    xt = h.reshape(B * S, D)
    N = xt.shape[0]
    logits = jnp.einsum('nd,de->ne', xt.astype(jnp.float32), w_router.astype(jnp.float32)) + b_router.astype(jnp.float32)
    top_v, top_i = lax.top_k(logits, TOP_K)
    gates = jax.nn.softmax(top_v, axis=-1)
    NK = N * TOP_K
    flat_e = top_i.reshape(NK)
    flat_t = jnp.arange(NK, dtype=jnp.int32) // TOP_K
    flat_g = gates.reshape(NK)
    order = jnp.argsort(flat_e)
    se = flat_e[order]
    counts = jnp.bincount(flat_e, length=N_EXPERTS)
    pcounts = (counts + MOE_BLOCK - 1) // MOE_BLOCK * MOE_BLOCK
    start = jnp.cumsum(counts) - counts
    pend = jnp.cumsum(pcounts)
    pstart = pend - pcounts
    dest = pstart[se] + jnp.arange(NK) - start[se]
    n_rows = -(-(NK + N_EXPERTS * (MOE_BLOCK - 1)) // MOE_BLOCK) * MOE_BLOCK
    n_blocks = n_rows // MOE_BLOCK
    row_tok = jnp.full((n_rows,), N, jnp.int32).at[dest].set(flat_t[order])
    row_g = jnp.zeros((n_rows,), jnp.float32).at[dest].set(flat_g[order])
    blk_e = jnp.minimum(jnp.searchsorted(pend, jnp.arange(n_blocks) * MOE_BLOCK, side='right'), N_EXPERTS - 1)
    x_pad = jnp.concatenate([xt, jnp.zeros((1, D), xt.dtype)], axis=0)
    xb = x_pad[row_tok].reshape(n_blocks, MOE_BLOCK, D)

    def expert_block(args):
        xblk, e = args
        g = xblk @ w_gate[e] + b_gate[e]
        u = xblk @ w_up[e] + b_up[e]
        g = jnp.minimum(g, SWIGLU_LIMIT)
        u = jnp.clip(u, -SWIGLU_LIMIT, SWIGLU_LIMIT)
        act = (u + 1) * g * jax.nn.sigmoid(SWIGLU_ALPHA * g)
        return act @ w_down[e] + b_down[e]

    yb = lax.map(expert_block, (xb, blk_e)).reshape(n_rows, D)
    y = jax.ops.segment_sum(yb.astype(jnp.float32) * row_g[:, None], row_tok, num_segments=N + 1)[:N]
    return y.astype(h.dtype).reshape(B, S, D)


def _modulations(c, w_ada, b_ada):
    m = jnp.einsum('bd,df->bf', jax.nn.silu(c), w_ada) + b_ada
    return jnp.split(m[:, None, :], 6, axis=-1)


def _modulate(x, g, shift, scale):
    return rmsnorm(x, g) * (1 + scale) + shift


def setup_inputs(seed: int = 0) -> dict:
    key = jax.random.key(seed)
    ks = jax.random.split(key, 32)
    f32 = jnp.float32
    wbuf = min(WINDOW_MAX, PAST_LEN)

    def nrm(k, shape, scale=1.0):
        return jax.random.normal(k, shape, f32) * scale

    L = DEPTH
    return {
        'x_prompt': nrm(ks[0], (BATCH, SEQ, D_MODEL)),
        'x_sample': nrm(ks[1], (DEC_BATCH, DEC_SEQ, D_MODEL)),
        'cache_k': nrm(ks[2], (L, DEC_BATCH, wbuf, N_ATT_HEADS, HEAD_DIM)),
        'cache_v': nrm(ks[3], (L, DEC_BATCH, wbuf, N_ATT_HEADS, HEAD_DIM)),
        'state_conv': nrm(ks[4], (L, DEC_BATCH, CONV_K - 1, CONV_WIDTH)),
        'c_prompt': nrm(ks[5], (BATCH, D_MODEL)),
        'c_sample': nrm(ks[6], (DEC_BATCH, D_MODEL)),
        'w_ada': nrm(ks[7], (L, D_MODEL, 6 * D_MODEL), D_MODEL ** -0.5),
        'b_ada': nrm(ks[8], (L, 6 * D_MODEL), 0.02),
        'g_norm_mix': 1.0 + nrm(ks[9], (L, D_MODEL), 0.02),
        'w_in': nrm(ks[10], (L, D_MODEL, IN_PROJ_WIDTH), D_MODEL ** -0.5),
        'conv_w': nrm(ks[11], (L, CONV_K, CONV_WIDTH), CONV_K ** -0.5),
        'g_attn_out': 1.0 + nrm(ks[12], (L, ATT_WIDTH), 0.02),
        'g_conv_out': 1.0 + nrm(ks[13], (L, CONV_WIDTH), 0.02),
        'w_out': nrm(ks[14], (L, D_MODEL, D_MODEL), D_MODEL ** -0.5),
        'g_norm_ffn': 1.0 + nrm(ks[15], (L, D_MODEL), 0.02),
        'w_router': nrm(ks[16], (L, D_MODEL, N_EXPERTS), D_MODEL ** -0.5),
        'b_router': nrm(ks[17], (L, N_EXPERTS), 0.01),
        'w_gate': nrm(ks[18], (L, N_EXPERTS, D_MODEL, D_FF), D_MODEL ** -0.5),
        'b_gate': nrm(ks[19], (L, N_EXPERTS, D_FF), 0.01),
        'w_up': nrm(ks[20], (L, N_EXPERTS, D_MODEL, D_FF), D_MODEL ** -0.5),
        'b_up': nrm(ks[21], (L, N_EXPERTS, D_FF), 0.01),
        'w_down': nrm(ks[22], (L, N_EXPERTS, D_FF, D_MODEL), D_FF ** -0.5),
        'b_down': nrm(ks[23], (L, N_EXPERTS, D_MODEL), 0.01),
        'g_final': 1.0 + nrm(ks[24], (D_MODEL,), 0.02),
    }


def reference(x_prompt, x_sample, cache_k, cache_v, state_conv, c_prompt, c_sample,
              w_ada, b_ada, g_norm_mix, w_in, conv_w, g_attn_out, g_conv_out, w_out,
              g_norm_ffn, w_router, b_router, w_gate, b_gate, w_up, b_up, w_down, b_down,
              g_final):
    xp, xs = x_prompt, x_sample
    kp_l, vp_l, cp_l, ks_l, vs_l, cs_l = [], [], [], [], [], []
    for l in range(DEPTH):
        moe_p = (w_router[l], b_router[l], w_gate[l], b_gate[l], w_up[l], b_up[l], w_down[l], b_down[l])
        mix_p = (w_in[l], conv_w[l], g_attn_out[l], g_conv_out[l], w_out[l])
        sh1, sc1, gt1, sh2, sc2, gt2 = _modulations(c_prompt, w_ada[l], b_ada[l])
        mix, kp, vp, cp = _prompt_mixer(_modulate(xp, g_norm_mix[l], sh1, sc1), *mix_p)
        xp = xp + gt1 * mix
        xp = xp + gt2 * _moe(_modulate(xp, g_norm_ffn[l], sh2, sc2), *moe_p)
        sh1, sc1, gt1, sh2, sc2, gt2 = _modulations(c_sample, w_ada[l], b_ada[l])
        mix, ksn, vsn, csn = _sample_mixer(_modulate(xs, g_norm_mix[l], sh1, sc1),
                                           cache_k[l], cache_v[l], state_conv[l], *mix_p)
        xs = xs + gt1 * mix
        xs = xs + gt2 * _moe(_modulate(xs, g_norm_ffn[l], sh2, sc2), *moe_p)
        kp_l.append(kp); vp_l.append(vp); cp_l.append(cp)
        ks_l.append(ksn); vs_l.append(vsn); cs_l.append(csn)
    y_prompt = rmsnorm(xp, g_final)
    y_sample = rmsnorm(xs, g_final)
    return (y_prompt, y_sample,
            jnp.stack(kp_l, 0), jnp.stack(vp_l, 0), jnp.stack(cp_l, 0),
            jnp.stack(ks_l, 0), jnp.stack(vs_l, 0), jnp.stack(cs_l, 0))
```

```python
import functools

import jax
import jax.numpy as jnp
import numpy as np
from jax import lax
from jax.experimental import pallas as pl
from jax.experimental.pallas import tpu as pltpu

F32 = jnp.float32
BF16 = jnp.bfloat16
I32 = jnp.int32
HIGHEST = lax.Precision.HIGHEST

D_MODEL = 1024
HEAD_DIM = 64
N_HEADS = 12
ATT_W = N_HEADS * HEAD_DIM
CONV_W = D_MODEL - ATT_W
CONV_K = 3
PATTERNS = ((128, 1), (512, 4), (2048, 16))
WINDOW_MAX = 2048
PAST_LEN = 16384
ROPE_THETA = 10000.0
N_EXPERTS = 32
TOP_K = 4
SWIGLU_ALPHA = 1.702
SWIGLU_LIMIT = 7.0
NORM_EPS = 1e-6
NEG = -1e30
IN_W = 3 * ATT_W + 3 * CONV_W

LANES = 128
Q_BLK = 128
NEAR_W = 512
NEAR_KEYS = NEAR_W + Q_BLK
FAR_D = 16
MOE_TM = 256
TOK_TM = 256
VMEM_LIMIT = 56 * 1024 * 1024


def _cparams(n_axes, vmem=VMEM_LIMIT):
    return pltpu.CompilerParams(dimension_semantics=("arbitrary",) * n_axes,
                                vmem_limit_bytes=vmem)


def _multiplicity(delta):
    delta = np.asarray(delta)
    c = np.zeros(delta.shape, np.float32)
    for w, d in PATTERNS:
        c += ((delta >= 0) & (delta <= w) & (delta % d == 0)).astype(np.float32)
    return c


def _ada_kernel(c_ref, w_ref, b_ref, o_ref):
    c = c_ref[...]
    s = c / (1.0 + jnp.exp(-c))
    o_ref[...] = jnp.dot(s, w_ref[...], precision=HIGHEST,
                         preferred_element_type=F32) + b_ref[...]


def _modulations(c_all, w_ada, b_ada):
    r, d = c_all.shape
    n = w_ada.shape[1]
    tn = 1536
    return pl.pallas_call(
        _ada_kernel,
        grid=(n // tn,),
        in_specs=[pl.BlockSpec((r, d), lambda j: (0, 0)),
                  pl.BlockSpec((d, tn), lambda j: (0, j)),
                  pl.BlockSpec((1, tn), lambda j: (0, j))],
        out_specs=pl.BlockSpec((r, tn), lambda j: (0, j)),
        out_shape=jax.ShapeDtypeStruct((r, n), F32),
        compiler_params=_cparams(1),
        name="ada_modulation",
    )(c_all, w_ada, b_ada.reshape(1, n))


def _norm_mod(x, g, shift, scale):
    ms = jnp.mean(x * x, axis=-1, keepdims=True)
    return (x * lax.rsqrt(ms + NORM_EPS) * g) * (1.0 + scale) + shift


def _rmsnorm(x, g):
    ms = jnp.mean(x * x, axis=-1, keepdims=True)
    return x * lax.rsqrt(ms + NORM_EPS) * g


def _swap_halves(xc):
    lane = lax.broadcasted_iota(I32, xc.shape, 1)
    first = (lane & (HEAD_DIM - 1)) < HEAD_DIM // 2
    return jnp.where(first, pltpu.roll(xc, LANES - HEAD_DIM // 2, 1),
                     pltpu.roll(xc, HEAD_DIM // 2, 1))


def _rope(x, cosf, sinf):
    outs = []
    for c in range(x.shape[1] // LANES):
        xc = x[:, c * LANES:(c + 1) * LANES]
        outs.append(xc * cosf + _swap_halves(xc) * sinf)
    return jnp.concatenate(outs, axis=1)


def _inproj_kernel(*refs, tm, sample, seq_per_batch):
    if sample:
        (x_ref, sh_ref, sc_ref, g_ref, w_ref, cos_ref, sin_ref, cw_ref, s1_ref, s2_ref,
         q_ref, k_ref, v_ref, conv_ref, cu_ref, cu_ext) = refs
    else:
        (x_ref, sh_ref, sc_ref, g_ref, w_ref, cos_ref, sin_ref, cw_ref,
         q_ref, k_ref, v_ref, conv_ref, tail_ref, cu_ext) = refs
    h = _norm_mod(x_ref[...], g_ref[...], sh_ref[...], sc_ref[...])
    z = jnp.dot(h.astype(BF16), w_ref[...], preferred_element_type=F32)
    cosf = cos_ref[...]
    sinf = sin_ref[...]
    q_ref[...] = _rope(z[:, 0:ATT_W], cosf, sinf) * (HEAD_DIM ** -0.5)
    k_ref[...] = _rope(z[:, ATT_W:2 * ATT_W], cosf, sinf)
    v_ref[...] = z[:, 2 * ATT_W:3 * ATT_W]
    o = 3 * ATT_W
    gb = z[:, o:o + CONV_W]
    cu = z[:, o + CONV_W:o + 2 * CONV_W] * z[:, o + 2 * CONV_W:o + 3 * CONV_W]
    if sample:
        cu_ext[0:8, :] = jnp.zeros((8, CONV_W), F32)
    else:
        @pl.when(pl.program_id(1) == 0)
        def _():
            cu_ext[0:8, :] = jnp.zeros((8, CONV_W), F32)
    cu_ext[8:8 + tm, :] = cu
    p1 = cu_ext[7:7 + tm, :]
    p2 = cu_ext[6:6 + tm, :]
    if sample:
        t = lax.broadcasted_iota(I32, (tm, CONV_W), 0) % seq_per_batch
        p1 = jnp.where(t >= 1, p1, 0.0) + s1_ref[...]
        p2 = jnp.where(t >= 2, p2, 0.0) + s2_ref[...]
        cu_ref[...] = cu
    cw = cw_ref[...]
    conv_ref[...] = gb * (cw[0:1, :] * p2 + cw[1:2, :] * p1 + cw[2:3, :] * cu)
    if not sample:
        tail = cu_ext[tm:tm + 8, :]
        tail_ref[...] = tail
        cu_ext[0:8, :] = tail


def _inproj_prompt(x, shift, scale, g, w_bf, cosf, sinf, conv_w, tm=512):
    b, s, d = x.shape
    row = lambda bi, j: (bi, j, 0)
    per_b = lambda bi, j: (bi, 0, 0)
    const = lambda bi, j: (0, 0)
    outs = pl.pallas_call(
        functools.partial(_inproj_kernel, tm=tm, sample=False, seq_per_batch=s),
        grid=(b, s // tm),
        in_specs=[pl.BlockSpec((None, tm, d), row),
                  pl.BlockSpec((None, 1, d), per_b),
                  pl.BlockSpec((None, 1, d), per_b),
                  pl.BlockSpec((1, d), const),
                  pl.BlockSpec((d, IN_W), const),
                  pl.BlockSpec((tm, LANES), lambda bi, j: (j, 0)),
                  pl.BlockSpec((tm, LANES), lambda bi, j: (j, 0)),
                  pl.BlockSpec((CONV_K, CONV_W), const)],
        out_specs=[pl.BlockSpec((None, tm, ATT_W), row),
                   pl.BlockSpec((None, tm, ATT_W), row),
                   pl.BlockSpec((None, tm, ATT_W), row),
                   pl.BlockSpec((None, tm, CONV_W), row),
                   pl.BlockSpec((None, 8, CONV_W), per_b)],
        out_shape=[jax.ShapeDtypeStruct((b, s, ATT_W), F32),
                   jax.ShapeDtypeStruct((b, s, ATT_W), F32),
                   jax.ShapeDtypeStruct((b, s, ATT_W), F32),
                   jax.ShapeDtypeStruct((b, s, CONV_W), F32),
                   jax.ShapeDtypeStruct((b, 8, CONV_W), F32)],
        scratch_shapes=[pltpu.VMEM((tm + 8, CONV_W), F32)],
        compiler_params=_cparams(2),
        name="inproj_prompt",
    )(x, shift, scale, g, w_bf, cosf, sinf, conv_w)
    return outs


def _inproj_sample(x, shift, scale, g, w_bf, cosf, sinf, conv_w, s1, s2, seq_per_batch):
    n, d = x.shape
    full = lambda shape: pl.BlockSpec(shape, lambda i: (0, 0))
    outs = pl.pallas_call(
        functools.partial(_inproj_kernel, tm=n, sample=True, seq_per_batch=seq_per_batch),
        grid=(1,),
        in_specs=[full((n, d)), full((n, d)), full((n, d)), full((1, d)), full((d, IN_W)),
                  full((n, LANES)), full((n, LANES)), full((CONV_K, CONV_W)),
                  full((n, CONV_W)), full((n, CONV_W))],
        out_specs=[full((n, ATT_W)), full((n, ATT_W)), full((n, ATT_W)),
                   full((n, CONV_W)), full((n, CONV_W))],
        out_shape=[jax.ShapeDtypeStruct((n, ATT_W), F32)] * 3
        + [jax.ShapeDtypeStruct((n, CONV_W), F32)] * 2,
        scratch_shapes=[pltpu.VMEM((n + 8, CONV_W), F32)],
        compiler_params=_cparams(1),
        name="inproj_sample",
    )(x, shift, scale, g, w_bf, cosf, sinf, conv_w, s1, s2)
    return outs


def _stack_heads(x):
    lo = lax.broadcasted_iota(I32, x.shape, 1) < HEAD_DIM
    zero = jnp.zeros_like(x)
    return jnp.concatenate([jnp.where(lo, x, zero), jnp.where(lo, zero, x)], axis=0)


def _split_pv(r0, r1, m, rows):
    lo = lax.broadcasted_iota(I32, r0.shape, 1) < HEAD_DIM
    num = jnp.where(lo, r0, r1)
    den = jnp.where(lo, pltpu.roll(r0, HEAD_DIM, 1), pltpu.roll(r1, HEAD_DIM, 1))
    mx = jnp.where(lo, jnp.broadcast_to(m[:rows], r0.shape), jnp.broadcast_to(m[rows:], r0.shape))
    return num, den, mx


def _attn_prompt_kernel(q_ref, k_ref, v_ref, bias_ref, mult_ref, o_ref,
                        kpad, v0pad, v1pad, num3, den3, max3, *, seq):
    nt = (((1,), (1,)), ((), ()))
    n_blk = seq // Q_BLK
    lo1 = lax.broadcasted_iota(I32, (Q_BLK, LANES), 1) < HEAD_DIM
    ones = jnp.ones((Q_BLK, LANES), F32)

    zpad = jnp.zeros((NEAR_W, LANES), BF16)
    kpad[0:NEAR_W, :] = zpad
    v0pad[0:NEAR_W, :] = zpad
    v1pad[0:NEAR_W, :] = zpad

    def fill(i, c):
        s0 = pl.multiple_of(i * Q_BLK, Q_BLK)
        d0 = pl.multiple_of(i * Q_BLK + NEAR_W, Q_BLK)
        kpad[pl.ds(d0, Q_BLK), :] = k_ref[pl.ds(s0, Q_BLK), :].astype(BF16)
        vb = v_ref[pl.ds(s0, Q_BLK), :]
        v0pad[pl.ds(d0, Q_BLK), :] = jnp.where(lo1, vb, ones).astype(BF16)
        v1pad[pl.ds(d0, Q_BLK), :] = jnp.where(lo1, ones, vb).astype(BF16)
        return c
    lax.fori_loop(0, n_blk, fill, 0)

    row = lax.broadcasted_iota(I32, (2 * Q_BLK, Q_BLK), 0) & (Q_BLK - 1)
    col = lax.broadcasted_iota(I32, (2 * Q_BLK, Q_BLK), 1)
    causal = col <= row

    def far(r, c):
        sl = pl.ds(r, seq // FAR_D, stride=FAR_D)
        q2 = _stack_heads(q_ref[sl, :]).astype(BF16)
        kr = k_ref[sl, :].astype(BF16)
        vr = v_ref[sl, :]
        s = lax.dot_general(q2, kr, nt, preferred_element_type=F32)
        s = jnp.where(causal, s, NEG)
        m = jnp.max(s, axis=1, keepdims=True)
        p = jnp.exp(s - m).astype(BF16)
        r0 = jnp.dot(p[:Q_BLK], jnp.where(lo1, vr, ones).astype(BF16), preferred_element_type=F32)
        r1 = jnp.dot(p[Q_BLK:], jnp.where(lo1, ones, vr).astype(BF16), preferred_element_type=F32)
        num, den, mx = _split_pv(r0, r1, m, Q_BLK)
        num3[sl, :] = num
        den3[sl, :] = den
        max3[sl, :] = mx
        return c
    lax.fori_loop(0, FAR_D, far, 0)

    kcol = lax.broadcasted_iota(I32, (1, NEAR_KEYS), 1)

    def near(i, c):
        s0 = pl.multiple_of(i * Q_BLK, Q_BLK)
        q2 = _stack_heads(q_ref[pl.ds(s0, Q_BLK), :]).astype(BF16)
        kw = kpad[pl.ds(s0, NEAR_KEYS), :]
        s = lax.dot_general(q2, kw, nt, preferred_element_type=F32) + bias_ref[...]
        s = jnp.where(kcol >= NEAR_W - s0, s, NEG)
        m = jnp.max(s, axis=1, keepdims=True)
        p = (jnp.exp(s - m) * mult_ref[...]).astype(BF16)
        r0 = jnp.dot(p[:Q_BLK], v0pad[pl.ds(s0, NEAR_KEYS), :], preferred_element_type=F32)
        r1 = jnp.dot(p[Q_BLK:], v1pad[pl.ds(s0, NEAR_KEYS), :], preferred_element_type=F32)
        num, den, mx = _split_pv(r0, r1, m, Q_BLK)
        mx3 = max3[pl.ds(s0, Q_BLK), :]
        mm = jnp.maximum(mx, mx3)
        a = jnp.exp(mx - mm)
        b = jnp.exp(mx3 - mm)
        o_ref[pl.ds(s0, Q_BLK), :] = ((num * a + num3[pl.ds(s0, Q_BLK), :] * b)
                                      / (den * a + den3[pl.ds(s0, Q_BLK), :] * b))
        return c
    lax.fori_loop(0, n_blk, near, 0)


def _near_tables():
    i = np.arange(Q_BLK)[:, None]
    kl = np.arange(NEAR_KEYS)[None, :]
    delta = i + NEAR_W - kl
    mult = np.zeros(delta.shape, np.float32)
    for w, d in PATTERNS[:2]:
        mult += ((delta >= 0) & (delta <= w) & (delta % d == 0)).astype(np.float32)
    bias = np.where(mult > 0, 0.0, NEG).astype(np.float32)
    return np.tile(bias, (2, 1)), np.tile(mult, (2, 1))


def _attn_prompt(q, k, v):
    b, s, _ = q.shape
    bias, mult = _near_tables()
    blk = pl.BlockSpec((None, s, LANES), lambda bi, hp: (bi, 0, hp))
    tab = pl.BlockSpec((2 * Q_BLK, NEAR_KEYS), lambda bi, hp: (0, 0))
    return pl.pallas_call(
        functools.partial(_attn_prompt_kernel, seq=s),
        grid=(b, ATT_W // LANES),
        in_specs=[blk, blk, blk, tab, tab],
        out_specs=blk,
        out_shape=jax.ShapeDtypeStruct((b, s, ATT_W), F32),
        scratch_shapes=[pltpu.VMEM((s + NEAR_W, LANES), BF16)] * 3
        + [pltpu.VMEM((s, LANES), F32)] * 3,
        compiler_params=_cparams(2),
        name="attn_prompt",
    )(q, k, v, jnp.asarray(bias), jnp.asarray(mult))


def _attn_sample_kernel(q_ref, kt_ref, vt_ref, knt_ref, vnt_ref, bias_ref, mult_ref, o_ref):
    nt = (((1,), (1,)), ((), ()))
    q = q_ref[...]
    bias = bias_ref[...]
    mult = mult_ref[...]
    n_c = kt_ref.shape[-1]
    for h in range(N_HEADS):
        qh = q[:, h * HEAD_DIM:(h + 1) * HEAD_DIM].astype(BF16)
        s = jnp.concatenate(
            [jnp.dot(qh, kt_ref[h].astype(BF16), preferred_element_type=F32),
             jnp.dot(qh, knt_ref[h].astype(BF16), preferred_element_type=F32)], axis=1) + bias
        m = jnp.max(s, axis=1, keepdims=True)
        p = jnp.exp(s - m) * mult
        den = jnp.sum(p, axis=1, keepdims=True)
        pb = p.astype(BF16)
        num = (lax.dot_general(pb[:, :n_c], vt_ref[h].astype(BF16), nt, preferred_element_type=F32)
               + lax.dot_general(pb[:, n_c:], vnt_ref[h].astype(BF16), nt, preferred_element_type=F32))
        o_ref[:, h * HEAD_DIM:(h + 1) * HEAD_DIM] = num / den


def _sample_tables(n_cache, t_new):
    t = np.arange(8)[:, None] % t_new
    rho = np.arange(n_cache)[None, :]
    c_cache = _multiplicity(t + n_cache - rho)
    tp = np.arange(LANES)[None, :]
    c_new = np.where(tp < t_new, _multiplicity(t - tp), 0.0)
    mult = np.concatenate([c_cache, c_new], axis=1).astype(np.float32)
    bias = np.where(mult > 0, 0.0, NEG).astype(np.float32)
    return bias, mult


def _attn_sample(q8, kt, vt, knt, vnt, t_new):
    b = q8.shape[0]
    n_c = kt.shape[-1]
    bias, mult = _sample_tables(n_c, t_new)
    cache = pl.BlockSpec((None, N_HEADS, HEAD_DIM, n_c), lambda i: (i, 0, 0, 0))
    new = pl.BlockSpec((None, N_HEADS, HEAD_DIM, LANES), lambda i: (i, 0, 0, 0))
    row = pl.BlockSpec((None, 8, ATT_W), lambda i: (i, 0, 0))
    tab = pl.BlockSpec((8, n_c + LANES), lambda i: (0, 0))
    return pl.pallas_call(
        _attn_sample_kernel,
        grid=(b,),
        in_specs=[row, cache, cache, new, new, tab, tab],
        out_specs=row,
        out_shape=jax.ShapeDtypeStruct((b, 8, ATT_W), F32),
        compiler_params=_cparams(1),
        name="attn_sample",
    )(q8, kt, vt, knt, vnt, jnp.asarray(bias), jnp.asarray(mult))


def _merge_kernel(att_ref, conv_ref, x_ref, gt1_ref, sh2_ref, sc2_ref, ga_ref, gc_ref, gf_ref,
                  wo_ref, wr_ref, br_ref, cnt0_ref, tri_ref,
                  x1_ref, h2_ref, idx_ref, gate_ref, rank_ref, cnt_ref, cnt_sc, *, n_axes):
    first = pl.program_id(0) == 0
    if n_axes == 2:
        first = jnp.logical_and(first, pl.program_id(1) == 0)

    @pl.when(first)
    def _():
        cnt_sc[...] = cnt0_ref[...]

    an = _rmsnorm(att_ref[...], ga_ref[...]).astype(BF16)
    cn = _rmsnorm(conv_ref[...], gc_ref[...]).astype(BF16)
    mix = (jnp.dot(an, wo_ref[0:ATT_W, :], preferred_element_type=F32)
           + jnp.dot(cn, wo_ref[ATT_W:D_MODEL, :], preferred_element_type=F32))
    x1 = x_ref[...] + gt1_ref[...] * mix
    x1_ref[...] = x1
    h2 = _norm_mod(x1, gf_ref[...], sh2_ref[...], sc2_ref[...])
    h2_ref[...] = h2
    logits = jnp.dot(h2, wr_ref[...], precision=HIGHEST, preferred_element_type=F32) + br_ref[...]
    tm = logits.shape[0]
    lane = lax.broadcasted_iota(I32, (tm, LANES), 1)
    work = logits
    vals, idxs = [], []
    for _ in range(TOP_K):
        mx = jnp.max(work, axis=1, keepdims=True)
        ix = jnp.min(jnp.where(work == mx, lane, LANES), axis=1, keepdims=True)
        vals.append(mx)
        idxs.append(ix)
        work = jnp.where(lane == ix, 3.0 * NEG, work)
    es = [jnp.exp(v - vals[0]) for v in vals]
    den = es[0] + es[1] + es[2] + es[3]
    onehot = jnp.zeros((tm, LANES), F32)
    for ix in idxs:
        onehot = onehot + (lane == ix).astype(F32)
    before = jnp.dot(tri_ref[...], onehot.astype(BF16), preferred_element_type=F32) + cnt_sc[0:1, :]
    idx_o = jnp.zeros((tm, LANES), I32)
    gate_o = jnp.zeros((tm, LANES), F32)
    rank_o = jnp.zeros((tm, LANES), F32)
    for t in range(TOP_K):
        rk = jnp.sum(jnp.where(lane == idxs[t], before, 0.0), axis=1, keepdims=True)
        idx_o = jnp.where(lane == t, idxs[t], idx_o)
        gate_o = jnp.where(lane == t, es[t] / den, gate_o)
        rank_o = jnp.where(lane == t, rk, rank_o)
    idx_ref[...] = idx_o
    gate_ref[...] = gate_o
    rank_ref[...] = rank_o.astype(I32)
    cnt_sc[...] = cnt_sc[...] + jnp.sum(onehot, axis=0, keepdims=True)
    cnt_ref[...] = cnt_sc[...]


def _merge(att, conv, x, gt1, sh2, sc2, ga, gc, gf, wo_bf, wr_pad, br_pad, cnt0, tm):
    d = D_MODEL
    tri = jnp.asarray(np.tril(np.ones((tm, tm), np.float32), -1), BF16)
    if att.ndim == 3:
        b, s, _ = att.shape
        grid = (b, s // tm)
        row = lambda w: pl.BlockSpec((None, tm, w), lambda bi, j: (bi, j, 0))
        mod = pl.BlockSpec((None, 1, d), lambda bi, j: (bi, 0, 0))
        const = lambda shape: pl.BlockSpec(shape, lambda bi, j: (0, 0))
        lead = (b, s)
    else:
        n = att.shape[0]
        grid = (n // tm,)
        row = lambda w: pl.BlockSpec((tm, w), lambda i: (i, 0))
        mod = row(d)
        const = lambda shape: pl.BlockSpec(shape, lambda i: (0, 0))
        lead = (n,)
    return pl.pallas_call(
        functools.partial(_merge_kernel, n_axes=len(grid)),
        grid=grid,
        in_specs=[row(ATT_W), row(CONV_W), row(d), mod, mod, mod,
                  const((1, ATT_W)), const((1, CONV_W)), const((1, d)),
                  const((d, d)), const((d, LANES)), const((1, LANES)), const((8, LANES)),
                  const((tm, tm))],
        out_specs=[row(d), row(d), row(LANES), row(LANES), row(LANES), const((8, LANES))],
        out_shape=[jax.ShapeDtypeStruct(lead + (d,), F32),
                   jax.ShapeDtypeStruct(lead + (d,), F32),
                   jax.ShapeDtypeStruct(lead + (LANES,), I32),
                   jax.ShapeDtypeStruct(lead + (LANES,), F32),
                   jax.ShapeDtypeStruct(lead + (LANES,), I32),
                   jax.ShapeDtypeStruct((8, LANES), F32)],
        scratch_shapes=[pltpu.VMEM((8, LANES), F32)],
        compiler_params=_cparams(len(grid)),
        name="merge_route",
    )(att, conv, x, gt1, sh2, sc2, ga, gc, gf, wo_bf, wr_pad, br_pad, cnt0, tri)


def _row_copies(dest_ref, tm, make):
    def start(n, c):
        for t in range(TOP_K):
            make(n, t, dest_ref[0, n * TOP_K + t]).start()
        return c
    lax.fori_loop(0, tm, start, 0)

    def wait(n, c):
        for t in range(TOP_K):
            make(n, t, dest_ref[0, n * TOP_K + t]).wait()
        return c
    lax.fori_loop(0, tm, wait, 0)


def _dispatch_kernel(ps_ref, pn_ref, t0_ref, dest_ref, h_ref, dest_s_ref, hs_ref, xs_ref,
                     zbuf, sem, *, tm, n_s, row_tm, n_tiles):
    def make(n, t, d):
        return pltpu.make_async_copy(h_ref.at[pl.ds(n, 1)], xs_ref.at[pl.ds(d, 1)], sem)
    _row_copies(dest_ref, tm, make)

    @pl.when(pl.program_id(0) == pl.num_programs(0) - 1)
    def _():
        def make_s(n, t, d):
            return pltpu.make_async_copy(hs_ref.at[pl.ds(n, 1)], xs_ref.at[pl.ds(d, 1)], sem)
        _row_copies(dest_s_ref, n_s, make_s)
        _zero_fill(ps_ref, pn_ref, t0_ref, xs_ref, zbuf, sem, row_tm, n_tiles)


def _dispatch(h2, dest, h2_s, dest_s, pad, n_rows, tm=TOK_TM, row_tm=MOE_TM):
    n, d = h2.shape
    n_s = h2_s.shape[0]
    nt = n // tm
    return pl.pallas_call(
        functools.partial(_dispatch_kernel, tm=tm, n_s=n_s, row_tm=row_tm, n_tiles=n_rows // row_tm),
        grid_spec=pltpu.PrefetchScalarGridSpec(
            num_scalar_prefetch=3,
            grid=(nt,),
            in_specs=[pl.BlockSpec((None, 1, tm * TOP_K), lambda i, *_: (i, 0, 0), memory_space=pltpu.SMEM),
                      pl.BlockSpec((tm, d), lambda i, *_: (i, 0)),
                      pl.BlockSpec((1, n_s * TOP_K), lambda i, *_: (0, 0), memory_space=pltpu.SMEM),
                      pl.BlockSpec((n_s, d), lambda i, *_: (0, 0))],
            out_specs=pl.BlockSpec(memory_space=pl.ANY),
            scratch_shapes=[pltpu.VMEM((row_tm, d), F32), pltpu.SemaphoreType.DMA(())],
        ),
        out_shape=jax.ShapeDtypeStruct((n_rows, d), F32),
        compiler_params=_cparams(1),
        name="moe_dispatch",
    )(*pad, dest.reshape(nt, 1, tm * TOP_K), h2, dest_s.reshape(1, n_s * TOP_K), h2_s)


def _zero_fill(ps_ref, pn_ref, t0_ref, xs_ref, zbuf, sem, tm, n_tiles):
    zbuf[...] = jnp.zeros_like(zbuf)
    sub = 8
    bits = [b for b in (tm >> (k + 1) for k in range(tm.bit_length() - 1)) if b >= sub]

    def segments(act):
        def seg(e, c):
            a = ps_ref[e]
            ln = pn_ref[e]
            head = jnp.minimum((-a) & (sub - 1), ln)
            for k in range(sub - 1):
                @pl.when(k < head)
                def _(k=k):
                    act(pltpu.make_async_copy(zbuf.at[pl.ds(0, 1)], xs_ref.at[pl.ds(a + k, 1)], sem))
            body = ln - head
            end = a + ln
            for bit in bits:
                @pl.when((body & bit) != 0)
                def _(end=end, bit=bit):
                    off = pl.multiple_of(end - bit, sub)
                    act(pltpu.make_async_copy(zbuf.at[pl.ds(0, bit)], xs_ref.at[pl.ds(off, bit)], sem))
                end = end - (body & bit)
            return c
        lax.fori_loop(0, N_EXPERTS, seg, 0)

        def tile(i, c):
            act(pltpu.make_async_copy(zbuf, xs_ref.at[pl.ds(pl.multiple_of(i * tm, tm), tm)], sem))
            return c
        lax.fori_loop(t0_ref[0], n_tiles, tile, 0)

    segments(lambda cp: cp.start())
    segments(lambda cp: cp.wait())


def _expert_kernel(te_ref, tv_ref, xs_ref, wg_ref, bg_ref, wu_ref, bu_ref, wd_ref, bd_ref,
                   ys_ref, wg_bf, wu_bf, wd_bf):
    i = pl.program_id(0)
    prev = te_ref[jnp.maximum(i - 1, 0)]
    new_expert = jnp.logical_or(i == 0, te_ref[i] != prev)

    @pl.when(new_expert)
    def _():
        wg_bf[...] = wg_ref[...].astype(BF16)
        wu_bf[...] = wu_ref[...].astype(BF16)
        wd_bf[...] = wd_ref[...].astype(BF16)

    @pl.when(tv_ref[i] > 0)
    def _():
        x = xs_ref[...].astype(BF16)
        g = jnp.dot(x, wg_bf[...], preferred_element_type=F32) + bg_ref[...]
        u = jnp.dot(x, wu_bf[...], preferred_element_type=F32) + bu_ref[...]
        g = jnp.minimum(g, SWIGLU_LIMIT)
        u = jnp.clip(u, -SWIGLU_LIMIT, SWIGLU_LIMIT)
        act = (u + 1.0) * g * (1.0 / (1.0 + jnp.exp(-SWIGLU_ALPHA * g)))
        ys_ref[...] = jnp.dot(act.astype(BF16), wd_bf[...], preferred_element_type=F32) + bd_ref[...]

    @pl.when(tv_ref[i] == 0)
    def _():
        ys_ref[...] = jnp.zeros_like(ys_ref)


def _experts(xs, tile_e, tile_valid, wg, bg, wu, bu, wd, bd, tm=MOE_TM):
    n_rows, d = xs.shape
    n_tiles = n_rows // tm
    f = wg.shape[-1]
    w_spec = lambda k, n: pl.BlockSpec((None, k, n), lambda i, te, tv: (te[i], 0, 0))
    b_spec = lambda n: pl.BlockSpec((None, 1, n), lambda i, te, tv: (te[i], 0, 0))
    rows = pl.BlockSpec((tm, d), lambda i, te, tv: (i, 0))
    return pl.pallas_call(
        _expert_kernel,
        grid_spec=pltpu.PrefetchScalarGridSpec(
            num_scalar_prefetch=2,
            grid=(n_tiles,),
            in_specs=[rows, w_spec(d, f), b_spec(f), w_spec(d, f), b_spec(f), w_spec(f, d), b_spec(d)],
            out_specs=rows,
            scratch_shapes=[pltpu.VMEM((d, f), BF16), pltpu.VMEM((d, f), BF16), pltpu.VMEM((f, d), BF16)],
        ),
        out_shape=jax.ShapeDtypeStruct((n_rows, d), F32),
        compiler_params=_cparams(1),
        name="moe_experts",
    )(tile_e, tile_valid, xs, wg, bg.reshape(N_EXPERTS, 1, f), wu, bu.reshape(N_EXPERTS, 1, f),
      wd, bd.reshape(N_EXPERTS, 1, d))


def _combine_kernel(dest_ref, ys_ref, x1_ref, gate_ref, gt2_ref, gfin_ref, o_ref, buf, sem, *, tm):
    def make(n, t, d):
        return pltpu.make_async_copy(ys_ref.at[pl.ds(d, 1)], buf.at[t, pl.ds(n, 1)], sem)
    _row_copies(dest_ref, tm, make)
    gate = gate_ref[...]
    y = gate[:, 0:1] * buf[0]
    for t in range(1, TOP_K):
        y = y + gate[:, t:t + 1] * buf[t]
    x2 = x1_ref[...] + gt2_ref[...] * y
    o_ref[...] = _rmsnorm(x2, gfin_ref[...])


def _combine(ys, dest, x1, gate, gt2, gfin, tm=TOK_TM):
    d = D_MODEL
    if x1.ndim == 3:
        b, s, _ = x1.shape
        tm = min(tm, s)
        nj = s // tm
        grid = (b, nj)
        row = lambda w: pl.BlockSpec((None, tm, w), lambda bi, j: (bi, j, 0))
        mod = pl.BlockSpec((None, 1, d), lambda bi, j: (bi, 0, 0))
        const = pl.BlockSpec((1, d), lambda bi, j: (0, 0))
        dspec = pl.BlockSpec((None, 1, tm * TOP_K), lambda bi, j: (bi * nj + j, 0, 0),
                             memory_space=pltpu.SMEM)
        n = b * s
    else:
        n = x1.shape[0]
        tm = min(tm, n)
        grid = (n // tm,)
        row = lambda w: pl.BlockSpec((tm, w), lambda i: (i, 0))
        mod = row(d)
        const = pl.BlockSpec((1, d), lambda i: (0, 0))
        dspec = pl.BlockSpec((None, 1, tm * TOP_K), lambda i: (i, 0, 0), memory_space=pltpu.SMEM)
    dest3 = dest.reshape(n // tm, 1, tm * TOP_K)
    return pl.pallas_call(
        functools.partial(_combine_kernel, tm=tm),
        grid=grid,
        in_specs=[dspec, pl.BlockSpec(memory_space=pl.ANY), row(d), row(LANES), mod, const],
        out_specs=row(d),
        out_shape=jax.ShapeDtypeStruct(x1.shape, F32),
        scratch_shapes=[pltpu.VMEM((TOP_K, tm, d), F32), pltpu.SemaphoreType.DMA(())],
        compiler_params=_cparams(len(grid)),
        name="moe_combine",
    )(dest3, ys, x1, gate, gt2, gfin)


def _routing_plan(counts, n_pairs, tm=MOE_TM):
    pc = (counts + tm - 1) // tm * tm
    pend = jnp.cumsum(pc)
    pstart = pend - pc
    n_rows = -(-(n_pairs + N_EXPERTS * (tm - 1)) // tm) * tm
    n_tiles = n_rows // tm
    tile_row = jnp.arange(n_tiles, dtype=I32) * tm
    last_used = jnp.max(jnp.where(pc > 0, jnp.arange(N_EXPERTS, dtype=I32), 0))
    tile_e = jnp.minimum(jnp.sum((tile_row[:, None] >= pend[None, :]).astype(I32), axis=1), last_used)
    tile_valid = (tile_row < pend[-1]).astype(I32)
    pad = ((pstart + counts).astype(I32), (pc - counts).astype(I32),
           (pend[-1:] // tm).astype(I32))
    return pstart.astype(I32), tile_e, tile_valid, n_rows, pad


def _rope_tables(pos):
    half = HEAD_DIM // 2
    inv = ROPE_THETA ** (-jnp.arange(half, dtype=F32) / half)
    ang = pos.astype(F32)[:, None] * inv[None, :]
    cos = jnp.cos(ang)
    sin = jnp.sin(ang)
    return jnp.concatenate([cos, cos, cos, cos], axis=1), jnp.concatenate([-sin, sin, -sin, sin], axis=1)


def kernel(x_prompt, x_sample, cache_k, cache_v, state_conv, c_prompt, c_sample, w_ada, b_ada,
           g_norm_mix, w_in, conv_w, g_attn_out, g_conv_out, w_out, g_norm_ffn, w_router, b_router,
           w_gate, b_gate, w_up, b_up, w_down, b_down, g_final):
    depth = w_in.shape[0]
    assert depth == 1, "single-layer trunk"
    bp, sp, d = x_prompt.shape
    bs, ts, _ = x_sample.shape
    ns = bs * ts
    n_cache = cache_k.shape[2]
    l = 0

    mods = _modulations(jnp.concatenate([c_prompt, c_sample], axis=0), w_ada[l], b_ada[l])
    mp = [m.reshape(bp, 1, d) for m in jnp.split(mods[:bp], 6, axis=-1)]
    ms = [jnp.repeat(m, ts, axis=0) for m in jnp.split(mods[bp:], 6, axis=-1)]

    w_in_bf = w_in[l].astype(BF16)
    w_out_bf = w_out[l].astype(BF16)
    wr_pad = jnp.pad(w_router[l], ((0, 0), (0, LANES - N_EXPERTS)))
    br_pad = jnp.pad(b_router[l].reshape(1, N_EXPERTS), ((0, 0), (0, LANES - N_EXPERTS)),
                     constant_values=NEG)
    g_mix = g_norm_mix[l].reshape(1, d)
    g_ffn = g_norm_ffn[l].reshape(1, d)
    g_att = g_attn_out[l].reshape(1, ATT_W)
    g_cnv = g_conv_out[l].reshape(1, CONV_W)
    g_fin = g_final.reshape(1, d)

    cos_p, sin_p = _rope_tables(jnp.arange(sp))
    q_p, k_p, v_p, conv_p, tail_p = _inproj_prompt(x_prompt, mp[0], mp[1], g_mix, w_in_bf,
                                                   cos_p, sin_p, conv_w[l])
    att_p = _attn_prompt(q_p, k_p, v_p)

    xs_rows = x_sample.reshape(ns, d)
    cos_s, sin_s = _rope_tables(PAST_LEN + jnp.arange(ts))
    cos_s = jnp.tile(cos_s, (bs, 1))
    sin_s = jnp.tile(sin_s, (bs, 1))
    st = state_conv[l]
    zrow = jnp.zeros((bs, 1, CONV_W), F32)
    s1 = jnp.concatenate([st[:, 1:2], zrow, zrow, zrow], axis=1).reshape(ns, CONV_W)
    s2 = jnp.concatenate([st[:, 0:1], st[:, 1:2], zrow, zrow], axis=1).reshape(ns, CONV_W)
    q_s, k_s, v_s, conv_s, cu_s = _inproj_sample(xs_rows, ms[0], ms[1], g_mix, w_in_bf,
                                                 cos_s, sin_s, conv_w[l], s1, s2, ts)
    to_t = lambda a: jnp.pad(jnp.transpose(a.reshape(bs, ts, N_HEADS, HEAD_DIM), (0, 2, 3, 1)),
                             ((0, 0), (0, 0), (0, 0), (0, LANES - ts)))
    q8 = jnp.pad(q_s.reshape(bs, ts, ATT_W), ((0, 0), (0, 8 - ts), (0, 0)))
    kt = jnp.transpose(cache_k[l], (0, 2, 3, 1))
    vt = jnp.transpose(cache_v[l], (0, 2, 3, 1))
    att_s = _attn_sample(q8, kt, vt, to_t(k_s), to_t(v_s), ts)[:, :ts].reshape(ns, ATT_W)

    cnt0 = jnp.zeros((8, LANES), F32)
    x1_p, h2_p, idx_p, gate_p, rank_p, cnt_p = _merge(
        att_p, conv_p, x_prompt, mp[2], mp[3], mp[4], g_att, g_cnv, g_ffn,
        w_out_bf, wr_pad, br_pad, cnt0, TOK_TM)
    x1_s, h2_s, idx_s, gate_s, rank_s, cnt_all = _merge(
        att_s, conv_s, xs_rows, ms[2], ms[3], ms[4], g_att, g_cnv, g_ffn,
        w_out_bf, wr_pad, br_pad, cnt_p, ns)

    n_tok = bp * sp + ns
    counts = cnt_all[0, :N_EXPERTS].astype(I32)
    pstart, tile_e, tile_valid, n_rows, pad = _routing_plan(counts, n_tok * TOP_K)
    dest_p = (jnp.take(pstart, idx_p[..., :TOP_K]) + rank_p[..., :TOP_K]).reshape(bp * sp, TOP_K)
    dest_s = jnp.take(pstart, idx_s[:, :TOP_K]) + rank_s[:, :TOP_K]
    xs_sorted = _dispatch(h2_p.reshape(bp * sp, d), dest_p, h2_s, dest_s, pad, n_rows)
    ys = _experts(xs_sorted, tile_e, tile_valid, w_gate[l], b_gate[l], w_up[l], b_up[l],
                  w_down[l], b_down[l])
    y_prompt = _combine(ys, dest_p, x1_p, gate_p, mp[5], g_fin)
    y_sample = _combine(ys, dest_s, x1_s, gate_s, ms[5], g_fin).reshape(bs, ts, d)

    heads = lambda a, b, s: a.reshape(1, b, s, N_HEADS, HEAD_DIM)
    keep = min(WINDOW_MAX, sp)
    return (y_prompt, y_sample,
            heads(k_p, bp, sp)[:, :, sp - keep:], heads(v_p, bp, sp)[:, :, sp - keep:],
            tail_p[:, 8 - (CONV_K - 1):][None],
            heads(k_s, bs, ts), heads(v_s, bs, ts),
            cu_s.reshape(bs, ts, CONV_W)[:, ts - (CONV_K - 1):][None])
```

```python
import functools

import jax
import jax.numpy as jnp
import numpy as np
from jax import lax
from jax.experimental import pallas as pl
from jax.experimental.pallas import tpu as pltpu

F32 = jnp.float32
BF16 = jnp.bfloat16
I32 = jnp.int32
HIGHEST = lax.Precision.HIGHEST

D_MODEL = 1024
HEAD_DIM = 64
N_HEADS = 12
ATT_W = N_HEADS * HEAD_DIM
CONV_W = D_MODEL - ATT_W
CONV_K = 3
PATTERNS = ((128, 1), (512, 4), (2048, 16))
WINDOW_MAX = 2048
PAST_LEN = 16384
ROPE_THETA = 10000.0
N_EXPERTS = 32
TOP_K = 4
SWIGLU_ALPHA = 1.702
SWIGLU_LIMIT = 7.0
NORM_EPS = 1e-6
NEG = -1e30
IN_W = 3 * ATT_W + 3 * CONV_W

LANES = 128
Q_BLK = 128
NEAR_W = 512
NEAR_KEYS = NEAR_W + Q_BLK
FAR_D = 16
MOE_TM = 256
TOK_TM = 256
VMEM_LIMIT = 56 * 1024 * 1024
TOK_ROWS = D_MODEL // LANES


def _cparams(n_axes, vmem=VMEM_LIMIT):
    return pltpu.CompilerParams(dimension_semantics=("arbitrary",) * n_axes,
                                vmem_limit_bytes=vmem)


def _multiplicity(delta):
    delta = np.asarray(delta)
    c = np.zeros(delta.shape, np.float32)
    for w, d in PATTERNS:
        c += ((delta >= 0) & (delta <= w) & (delta % d == 0)).astype(np.float32)
    return c


def _ada_kernel(c_ref, w_ref, b_ref, o_ref):
    c = c_ref[...]
    s = c / (1.0 + jnp.exp(-c))
    o_ref[...] = jnp.dot(s, w_ref[...], precision=HIGHEST,
                         preferred_element_type=F32) + b_ref[...]


def _modulations(c_all, w_ada, b_ada):
    r, d = c_all.shape
    n = w_ada.shape[1]
    tn = 1536
    return pl.pallas_call(
        _ada_kernel,
        grid=(n // tn,),
        in_specs=[pl.BlockSpec((r, d), lambda j: (0, 0)),
                  pl.BlockSpec((d, tn), lambda j: (0, j)),
                  pl.BlockSpec((1, tn), lambda j: (0, j))],
        out_specs=pl.BlockSpec((r, tn), lambda j: (0, j)),
        out_shape=jax.ShapeDtypeStruct((r, n), F32),
        compiler_params=_cparams(1),
        name="ada_modulation",
    )(c_all, w_ada, b_ada.reshape(1, n))


def _norm_mod(x, g, shift, scale):
    ms = jnp.mean(x * x, axis=-1, keepdims=True)
    return (x * lax.rsqrt(ms + NORM_EPS) * g) * (1.0 + scale) + shift


def _rmsnorm(x, g):
    ms = jnp.mean(x * x, axis=-1, keepdims=True)
    return x * lax.rsqrt(ms + NORM_EPS) * g


def _store_token_tiles(ref, x, base=0):
    tm = x.shape[0]
    for c in range(TOK_ROWS):
        ref[pl.ds(base + c, tm, stride=TOK_ROWS), :] = x[:, c * LANES:(c + 1) * LANES]


def _load_token_tiles(ref, tm, base=0):
    return jnp.concatenate([ref[pl.ds(base + c, tm, stride=TOK_ROWS), :] for c in range(TOK_ROWS)],
                           axis=1)


def _swap_halves(xc):
    lane = lax.broadcasted_iota(I32, xc.shape, 1)
    first = (lane & (HEAD_DIM - 1)) < HEAD_DIM // 2
    return jnp.where(first, pltpu.roll(xc, LANES - HEAD_DIM // 2, 1),
                     pltpu.roll(xc, HEAD_DIM // 2, 1))


def _rope(x, cosf, sinf):
    outs = []
    for c in range(x.shape[1] // LANES):
        xc = x[:, c * LANES:(c + 1) * LANES]
        outs.append(xc * cosf + _swap_halves(xc) * sinf)
    return jnp.concatenate(outs, axis=1)


def _inproj_kernel(*refs, tm, sample, seq_per_batch):
    if sample:
        (x_ref, sh_ref, sc_ref, g_ref, w_ref, cos_ref, sin_ref, cw_ref, s1_ref, s2_ref,
         q_ref, k_ref, v_ref, conv_ref, cu_ref, cu_ext) = refs
    else:
        (x_ref, sh_ref, sc_ref, g_ref, w_ref, cos_ref, sin_ref, cw_ref,
         q_ref, k_ref, v_ref, conv_ref, tail_ref, cu_ext) = refs
    h = _norm_mod(x_ref[...], g_ref[...], sh_ref[...], sc_ref[...])
    z = jnp.dot(h.astype(BF16), w_ref[...], preferred_element_type=F32)
    cosf = cos_ref[...]
    sinf = sin_ref[...]
    q_ref[...] = _rope(z[:, 0:ATT_W], cosf, sinf) * (HEAD_DIM ** -0.5)
    k_ref[...] = _rope(z[:, ATT_W:2 * ATT_W], cosf, sinf)
    v_ref[...] = z[:, 2 * ATT_W:3 * ATT_W]
    o = 3 * ATT_W
    gb = z[:, o:o + CONV_W]
    cu = z[:, o + CONV_W:o + 2 * CONV_W] * z[:, o + 2 * CONV_W:o + 3 * CONV_W]
    if sample:
        cu_ext[0:8, :] = jnp.zeros((8, CONV_W), F32)
    else:
        @pl.when(pl.program_id(1) == 0)
        def _():
            cu_ext[0:8, :] = jnp.zeros((8, CONV_W), F32)
    cu_ext[8:8 + tm, :] = cu
    p1 = cu_ext[7:7 + tm, :]
    p2 = cu_ext[6:6 + tm, :]
    if sample:
        t = lax.broadcasted_iota(I32, (tm, CONV_W), 0) % seq_per_batch
        p1 = jnp.where(t >= 1, p1, 0.0) + s1_ref[...]
        p2 = jnp.where(t >= 2, p2, 0.0) + s2_ref[...]
        cu_ref[...] = cu
    cw = cw_ref[...]
    conv_ref[...] = gb * (cw[0:1, :] * p2 + cw[1:2, :] * p1 + cw[2:3, :] * cu)
    if not sample:
        tail = cu_ext[tm:tm + 8, :]
        tail_ref[...] = tail
        cu_ext[0:8, :] = tail


def _inproj_prompt(x, shift, scale, g, w_bf, cosf, sinf, conv_w, tm=512):
    b, s, d = x.shape
    row = lambda bi, j: (bi, j, 0)
    per_b = lambda bi, j: (bi, 0, 0)
    const = lambda bi, j: (0, 0)
    outs = pl.pallas_call(
        functools.partial(_inproj_kernel, tm=tm, sample=False, seq_per_batch=s),
        grid=(b, s // tm),
        in_specs=[pl.BlockSpec((None, tm, d), row),
                  pl.BlockSpec((None, 1, d), per_b),
                  pl.BlockSpec((None, 1, d), per_b),
                  pl.BlockSpec((1, d), const),
                  pl.BlockSpec((d, IN_W), const),
                  pl.BlockSpec((tm, LANES), lambda bi, j: (j, 0)),
                  pl.BlockSpec((tm, LANES), lambda bi, j: (j, 0)),
                  pl.BlockSpec((CONV_K, CONV_W), const)],
        out_specs=[pl.BlockSpec((None, tm, ATT_W), row),
                   pl.BlockSpec((None, tm, ATT_W), row),
                   pl.BlockSpec((None, tm, ATT_W), row),
                   pl.BlockSpec((None, tm, CONV_W), row),
                   pl.BlockSpec((None, 8, CONV_W), per_b)],
        out_shape=[jax.ShapeDtypeStruct((b, s, ATT_W), F32),
                   jax.ShapeDtypeStruct((b, s, ATT_W), F32),
                   jax.ShapeDtypeStruct((b, s, ATT_W), F32),
                   jax.ShapeDtypeStruct((b, s, CONV_W), F32),
                   jax.ShapeDtypeStruct((b, 8, CONV_W), F32)],
        scratch_shapes=[pltpu.VMEM((tm + 8, CONV_W), F32)],
        compiler_params=_cparams(2),
        name="inproj_prompt",
    )(x, shift, scale, g, w_bf, cosf, sinf, conv_w)
    return outs


def _inproj_sample(x, shift, scale, g, w_bf, cosf, sinf, conv_w, s1, s2, seq_per_batch):
    n, d = x.shape
    full = lambda shape: pl.BlockSpec(shape, lambda i: (0, 0))
    outs = pl.pallas_call(
        functools.partial(_inproj_kernel, tm=n, sample=True, seq_per_batch=seq_per_batch),
        grid=(1,),
        in_specs=[full((n, d)), full((n, d)), full((n, d)), full((1, d)), full((d, IN_W)),
                  full((n, LANES)), full((n, LANES)), full((CONV_K, CONV_W)),
                  full((n, CONV_W)), full((n, CONV_W))],
        out_specs=[full((n, ATT_W)), full((n, ATT_W)), full((n, ATT_W)),
                   full((n, CONV_W)), full((n, CONV_W))],
        out_shape=[jax.ShapeDtypeStruct((n, ATT_W), F32)] * 3
        + [jax.ShapeDtypeStruct((n, CONV_W), F32)] * 2,
        scratch_shapes=[pltpu.VMEM((n + 8, CONV_W), F32)],
        compiler_params=_cparams(1),
        name="inproj_sample",
    )(x, shift, scale, g, w_bf, cosf, sinf, conv_w, s1, s2)
    return outs


def _stack_heads(x):
    lo = lax.broadcasted_iota(I32, x.shape, 1) < HEAD_DIM
    zero = jnp.zeros_like(x)
    return jnp.concatenate([jnp.where(lo, x, zero), jnp.where(lo, zero, x)], axis=0)


def _split_pv(r0, r1, m, rows):
    lo = lax.broadcasted_iota(I32, r0.shape, 1) < HEAD_DIM
    num = jnp.where(lo, r0, r1)
    den = jnp.where(lo, pltpu.roll(r0, HEAD_DIM, 1), pltpu.roll(r1, HEAD_DIM, 1))
    mx = jnp.where(lo, jnp.broadcast_to(m[:rows], r0.shape), jnp.broadcast_to(m[rows:], r0.shape))
    return num, den, mx


def _attn_prompt_kernel(q_ref, k_ref, v_ref, bias_ref, mult_ref, o_ref,
                        kpad, v0pad, v1pad, num3, den3, max3, *, seq, far_unroll, near_unroll):
    nt = (((1,), (1,)), ((), ()))
    n_blk = seq // Q_BLK
    lo1 = lax.broadcasted_iota(I32, (Q_BLK, LANES), 1) < HEAD_DIM
    ones = jnp.ones((Q_BLK, LANES), F32)

    zpad = jnp.zeros((NEAR_W, LANES), BF16)
    kpad[0:NEAR_W, :] = zpad
    v0pad[0:NEAR_W, :] = zpad
    v1pad[0:NEAR_W, :] = zpad

    def fill(i, c):
        s0 = pl.multiple_of(i * Q_BLK, Q_BLK)
        d0 = pl.multiple_of(i * Q_BLK + NEAR_W, Q_BLK)
        kpad[pl.ds(d0, Q_BLK), :] = k_ref[pl.ds(s0, Q_BLK), :].astype(BF16)
        vb = v_ref[pl.ds(s0, Q_BLK), :]
        v0pad[pl.ds(d0, Q_BLK), :] = jnp.where(lo1, vb, ones).astype(BF16)
        v1pad[pl.ds(d0, Q_BLK), :] = jnp.where(lo1, ones, vb).astype(BF16)
        return c
    lax.fori_loop(0, n_blk, fill, 0)

    row = lax.broadcasted_iota(I32, (2 * Q_BLK, Q_BLK), 0) & (Q_BLK - 1)
    col = lax.broadcasted_iota(I32, (2 * Q_BLK, Q_BLK), 1)
    causal = col <= row

    def far(r, c):
        sl = pl.ds(r, seq // FAR_D, stride=FAR_D)
        q2 = _stack_heads(q_ref[sl, :]).astype(BF16)
        kr = k_ref[sl, :].astype(BF16)
        vr = v_ref[sl, :]
        s = lax.dot_general(q2, kr, nt, preferred_element_type=F32)
        s = jnp.where(causal, s, NEG)
        m = jnp.max(s, axis=1, keepdims=True)
        p = jnp.exp(s - m).astype(BF16)
        r0 = jnp.dot(p[:Q_BLK], jnp.where(lo1, vr, ones).astype(BF16), preferred_element_type=F32)
        r1 = jnp.dot(p[Q_BLK:], jnp.where(lo1, ones, vr).astype(BF16), preferred_element_type=F32)
        num, den, mx = _split_pv(r0, r1, m, Q_BLK)
        num3[sl, :] = num
        den3[sl, :] = den
        max3[sl, :] = mx
        return c
    lax.fori_loop(0, FAR_D, far, 0, unroll=far_unroll)

    kcol = lax.broadcasted_iota(I32, (1, NEAR_KEYS), 1)

    def near(i, c):
        s0 = pl.multiple_of(i * Q_BLK, Q_BLK)
        q2 = _stack_heads(q_ref[pl.ds(s0, Q_BLK), :]).astype(BF16)
        kw = kpad[pl.ds(s0, NEAR_KEYS), :]
        s = lax.dot_general(q2, kw, nt, preferred_element_type=F32) + bias_ref[...]
        s = jnp.where(kcol >= NEAR_W - s0, s, NEG)
        m = jnp.max(s, axis=1, keepdims=True)
        p = (jnp.exp(s - m) * mult_ref[...]).astype(BF16)
        r0 = jnp.dot(p[:Q_BLK], v0pad[pl.ds(s0, NEAR_KEYS), :], preferred_element_type=F32)
        r1 = jnp.dot(p[Q_BLK:], v1pad[pl.ds(s0, NEAR_KEYS), :], preferred_element_type=F32)
        num, den, mx = _split_pv(r0, r1, m, Q_BLK)
        mx3 = max3[pl.ds(s0, Q_BLK), :]
        mm = jnp.maximum(mx, mx3)
        a = jnp.exp(mx - mm)
        b = jnp.exp(mx3 - mm)
        o_ref[pl.ds(s0, Q_BLK), :] = ((num * a + num3[pl.ds(s0, Q_BLK), :] * b)
                                      / (den * a + den3[pl.ds(s0, Q_BLK), :] * b))
        return c
    lax.fori_loop(0, n_blk, near, 0, unroll=near_unroll)


def _near_tables():
    i = np.arange(Q_BLK)[:, None]
    kl = np.arange(NEAR_KEYS)[None, :]
    delta = i + NEAR_W - kl
    mult = np.zeros(delta.shape, np.float32)
    for w, d in PATTERNS[:2]:
        mult += ((delta >= 0) & (delta <= w) & (delta % d == 0)).astype(np.float32)
    bias = np.where(mult > 0, 0.0, NEG).astype(np.float32)
    return np.tile(bias, (2, 1)), np.tile(mult, (2, 1))


def _attn_prompt(q, k, v, far_unroll=4, near_unroll=2):
    b, s, _ = q.shape
    bias, mult = _near_tables()
    blk = pl.BlockSpec((None, s, LANES), lambda bi, hp: (bi, 0, hp))
    tab = pl.BlockSpec((2 * Q_BLK, NEAR_KEYS), lambda bi, hp: (0, 0))
    return pl.pallas_call(
        functools.partial(_attn_prompt_kernel, seq=s, far_unroll=far_unroll, near_unroll=near_unroll),
        grid=(b, ATT_W // LANES),
        in_specs=[blk, blk, blk, tab, tab],
        out_specs=blk,
        out_shape=jax.ShapeDtypeStruct((b, s, ATT_W), F32),
        scratch_shapes=[pltpu.VMEM((s + NEAR_W, LANES), BF16)] * 3
        + [pltpu.VMEM((s, LANES), F32)] * 3,
        compiler_params=_cparams(2),
        name="attn_prompt",
    )(q, k, v, jnp.asarray(bias), jnp.asarray(mult))


def _attn_sample_kernel(q_ref, kt_ref, vt_ref, knt_ref, vnt_ref, bias_ref, mult_ref, o_ref):
    nt = (((1,), (1,)), ((), ()))
    q = q_ref[...]
    bias = bias_ref[...]
    mult = mult_ref[...]
    n_c = kt_ref.shape[-1]
    for h in range(N_HEADS):
        qh = q[:, h * HEAD_DIM:(h + 1) * HEAD_DIM].astype(BF16)
        s = jnp.concatenate(
            [jnp.dot(qh, kt_ref[h].astype(BF16), preferred_element_type=F32),
             jnp.dot(qh, knt_ref[h].astype(BF16), preferred_element_type=F32)], axis=1) + bias
        m = jnp.max(s, axis=1, keepdims=True)
        p = jnp.exp(s - m) * mult
        den = jnp.sum(p, axis=1, keepdims=True)
        pb = p.astype(BF16)
        num = (lax.dot_general(pb[:, :n_c], vt_ref[h].astype(BF16), nt, preferred_element_type=F32)
               + lax.dot_general(pb[:, n_c:], vnt_ref[h].astype(BF16), nt, preferred_element_type=F32))
        o_ref[:, h * HEAD_DIM:(h + 1) * HEAD_DIM] = num / den


def _sample_tables(n_cache, t_new):
    t = np.arange(8)[:, None] % t_new
    rho = np.arange(n_cache)[None, :]
    c_cache = _multiplicity(t + n_cache - rho)
    tp = np.arange(LANES)[None, :]
    c_new = np.where(tp < t_new, _multiplicity(t - tp), 0.0)
    mult = np.concatenate([c_cache, c_new], axis=1).astype(np.float32)
    bias = np.where(mult > 0, 0.0, NEG).astype(np.float32)
    return bias, mult


def _attn_sample(q8, kt, vt, knt, vnt, t_new):
    b = q8.shape[0]
    n_c = kt.shape[-1]
    bias, mult = _sample_tables(n_c, t_new)
    cache = pl.BlockSpec((None, N_HEADS, HEAD_DIM, n_c), lambda i: (i, 0, 0, 0))
    new = pl.BlockSpec((None, N_HEADS, HEAD_DIM, LANES), lambda i: (i, 0, 0, 0))
    row = pl.BlockSpec((None, 8, ATT_W), lambda i: (i, 0, 0))
    tab = pl.BlockSpec((8, n_c + LANES), lambda i: (0, 0))
    return pl.pallas_call(
        _attn_sample_kernel,
        grid=(b,),
        in_specs=[row, cache, cache, new, new, tab, tab],
        out_specs=row,
        out_shape=jax.ShapeDtypeStruct((b, 8, ATT_W), F32),
        compiler_params=_cparams(1),
        name="attn_sample",
    )(q8, kt, vt, knt, vnt, jnp.asarray(bias), jnp.asarray(mult))


def _merge_kernel(att_ref, conv_ref, x_ref, gt1_ref, sh2_ref, sc2_ref, ga_ref, gc_ref, gf_ref,
                  wo_ref, wr_ref, br_ref, cnt0_ref, tri_ref,
                  x1_ref, h2_ref, idx_ref, gate_ref, rank_ref, cnt_ref, cnt_sc, *, n_axes):
    first = pl.program_id(0) == 0
    if n_axes == 2:
        first = jnp.logical_and(first, pl.program_id(1) == 0)

    @pl.when(first)
    def _():
        cnt_sc[...] = cnt0_ref[...]

    an = _rmsnorm(att_ref[...], ga_ref[...]).astype(BF16)
    cn = _rmsnorm(conv_ref[...], gc_ref[...]).astype(BF16)
    mix = (jnp.dot(an, wo_ref[0:ATT_W, :], preferred_element_type=F32)
           + jnp.dot(cn, wo_ref[ATT_W:D_MODEL, :], preferred_element_type=F32))
    x1 = x_ref[...] + gt1_ref[...] * mix
    x1_ref[...] = x1
    h2 = _norm_mod(x1, gf_ref[...], sh2_ref[...], sc2_ref[...])
    _store_token_tiles(h2_ref, h2)
    logits = jnp.dot(h2, wr_ref[...], precision=HIGHEST, preferred_element_type=F32) + br_ref[...]
    tm = logits.shape[0]
    lane = lax.broadcasted_iota(I32, (tm, LANES), 1)
    work = logits
    vals, idxs = [], []
    for _ in range(TOP_K):
        mx = jnp.max(work, axis=1, keepdims=True)
        ix = jnp.min(jnp.where(work == mx, lane, LANES), axis=1, keepdims=True)
        vals.append(mx)
        idxs.append(ix)
        work = jnp.where(lane == ix, 3.0 * NEG, work)
    es = [jnp.exp(v - vals[0]) for v in vals]
    den = es[0] + es[1] + es[2] + es[3]
    onehot = jnp.zeros((tm, LANES), F32)
    for ix in idxs:
        onehot = onehot + (lane == ix).astype(F32)
    before = jnp.dot(tri_ref[...], onehot.astype(BF16), preferred_element_type=F32) + cnt_sc[0:1, :]
    idx_o = jnp.zeros((tm, LANES), I32)
    gate_o = jnp.zeros((tm, LANES), F32)
    rank_o = jnp.zeros((tm, LANES), F32)
    for t in range(TOP_K):
        rk = jnp.sum(jnp.where(lane == idxs[t], before, 0.0), axis=1, keepdims=True)
        idx_o = jnp.where(lane == t, idxs[t], idx_o)
        gate_o = jnp.where(lane == t, es[t] / den, gate_o)
        rank_o = jnp.where(lane == t, rk, rank_o)
    idx_ref[...] = idx_o
    gate_ref[...] = gate_o
    rank_ref[...] = rank_o.astype(I32)
    cnt_sc[...] = cnt_sc[...] + jnp.sum(onehot, axis=0, keepdims=True)
    cnt_ref[...] = cnt_sc[...]


def _merge(att, conv, x, gt1, sh2, sc2, ga, gc, gf, wo_bf, wr_pad, br_pad, cnt0, tm):
    d = D_MODEL
    tri = jnp.asarray(np.tril(np.ones((tm, tm), np.float32), -1), BF16)
    if att.ndim == 3:
        b, s, _ = att.shape
        grid = (b, s // tm)
        row = lambda w: pl.BlockSpec((None, tm, w), lambda bi, j: (bi, j, 0))
        mod = pl.BlockSpec((None, 1, d), lambda bi, j: (bi, 0, 0))
        const = lambda shape: pl.BlockSpec(shape, lambda bi, j: (0, 0))
        lead = (b, s)
        tiles = pl.BlockSpec((None, tm * TOK_ROWS, LANES), lambda bi, j: (bi, j, 0))
        tiles_shape = (b, s * TOK_ROWS, LANES)
    else:
        n = att.shape[0]
        grid = (n // tm,)
        row = lambda w: pl.BlockSpec((tm, w), lambda i: (i, 0))
        mod = row(d)
        const = lambda shape: pl.BlockSpec(shape, lambda i: (0, 0))
        lead = (n,)
        tiles = pl.BlockSpec((tm * TOK_ROWS, LANES), lambda i: (i, 0))
        tiles_shape = (n * TOK_ROWS, LANES)
    return pl.pallas_call(
        functools.partial(_merge_kernel, n_axes=len(grid)),
        grid=grid,
        in_specs=[row(ATT_W), row(CONV_W), row(d), mod, mod, mod,
                  const((1, ATT_W)), const((1, CONV_W)), const((1, d)),
                  const((d, d)), const((d, LANES)), const((1, LANES)), const((8, LANES)),
                  const((tm, tm))],
        out_specs=[row(d), tiles, row(LANES), row(LANES), row(LANES), const((8, LANES))],
        out_shape=[jax.ShapeDtypeStruct(lead + (d,), F32),
                   jax.ShapeDtypeStruct(tiles_shape, F32),
                   jax.ShapeDtypeStruct(lead + (LANES,), I32),
                   jax.ShapeDtypeStruct(lead + (LANES,), F32),
                   jax.ShapeDtypeStruct(lead + (LANES,), I32),
                   jax.ShapeDtypeStruct((8, LANES), F32)],
        scratch_shapes=[pltpu.VMEM((8, LANES), F32)],
        compiler_params=_cparams(len(grid)),
        name="merge_route",
    )(att, conv, x, gt1, sh2, sc2, ga, gc, gf, wo_bf, wr_pad, br_pad, cnt0, tri)


def _tok(ref, n, count=1):
    return ref.at[pl.ds(pl.multiple_of(n * TOK_ROWS, TOK_ROWS), count * TOK_ROWS)]


def _row_copies(dest_ref, tm, make):
    def start(n, c):
        for t in range(TOP_K):
            make(n, t, dest_ref[0, n * TOP_K + t]).start(priority=t % 2)
        return c
    lax.fori_loop(0, tm, start, 0)

    def wait(n, c):
        for t in range(TOP_K):
            make(n, t, dest_ref[0, n * TOP_K + t]).wait()
        return c
    lax.fori_loop(0, tm, wait, 0)


def _dispatch_kernel(ps_ref, pn_ref, t0_ref, dest_ref, h_ref, dest_s_ref, hs_ref, xs_ref,
                     zbuf, sem, *, tm, n_s, row_tm, n_tiles):
    def make(n, t, d):
        return pltpu.make_async_copy(_tok(h_ref, n), _tok(xs_ref, d), sem)
    _row_copies(dest_ref, tm, make)

    @pl.when(pl.program_id(0) == pl.num_programs(0) - 1)
    def _():
        def make_s(n, t, d):
            return pltpu.make_async_copy(_tok(hs_ref, n), _tok(xs_ref, d), sem)
        _row_copies(dest_s_ref, n_s, make_s)
        _zero_fill(ps_ref, pn_ref, t0_ref, xs_ref, zbuf, sem, row_tm, n_tiles)


def _dispatch(h2, dest, h2_s, dest_s, pad, n_rows, tm=TOK_TM, row_tm=MOE_TM):
    n = h2.shape[0] // TOK_ROWS
    n_s = h2_s.shape[0] // TOK_ROWS
    nt = n // tm
    return pl.pallas_call(
        functools.partial(_dispatch_kernel, tm=tm, n_s=n_s, row_tm=row_tm, n_tiles=n_rows // row_tm),
        grid_spec=pltpu.PrefetchScalarGridSpec(
            num_scalar_prefetch=3,
            grid=(nt,),
            in_specs=[pl.BlockSpec((None, 1, tm * TOP_K), lambda i, *_: (i, 0, 0), memory_space=pltpu.SMEM),
                      pl.BlockSpec((tm * TOK_ROWS, LANES), lambda i, *_: (i, 0)),
                      pl.BlockSpec((1, n_s * TOP_K), lambda i, *_: (0, 0), memory_space=pltpu.SMEM),
                      pl.BlockSpec((n_s * TOK_ROWS, LANES), lambda i, *_: (0, 0))],
            out_specs=pl.BlockSpec(memory_space=pl.ANY),
            scratch_shapes=[pltpu.VMEM((row_tm * TOK_ROWS, LANES), F32), pltpu.SemaphoreType.DMA(())],
        ),
        out_shape=jax.ShapeDtypeStruct((n_rows * TOK_ROWS, LANES), F32),
        compiler_params=_cparams(1),
        name="moe_dispatch",
    )(*pad, dest.reshape(nt, 1, tm * TOP_K), h2, dest_s.reshape(1, n_s * TOP_K), h2_s)


def _zero_fill(ps_ref, pn_ref, t0_ref, xs_ref, zbuf, sem, tm, n_tiles):
    zbuf[...] = jnp.zeros_like(zbuf)
    bits = [tm >> (k + 1) for k in range(tm.bit_length() - 1)]

    def segments(act):
        def seg(e, c):
            off = ps_ref[e]
            ln = pn_ref[e]
            for bit in bits:
                @pl.when((ln & bit) != 0)
                def _(off=off, bit=bit):
                    act(pltpu.make_async_copy(_tok(zbuf, 0, bit), _tok(xs_ref, off, bit), sem))
                off = off + (ln & bit)
            return c
        lax.fori_loop(0, N_EXPERTS, seg, 0)

        def tile(i, c):
            act(pltpu.make_async_copy(zbuf, _tok(xs_ref, i * tm, tm), sem))
            return c
        lax.fori_loop(t0_ref[0], n_tiles, tile, 0)

    segments(lambda cp: cp.start())
    segments(lambda cp: cp.wait())


def _expert_kernel(te_ref, tv_ref, nx_ref, sl_ref, xs_ref, wg_hbm, bg_ref, wu_hbm, bu_ref,
                   wd_hbm, bd_ref, ys_ref, wbuf, wg_bf, wu_bf, wd_bf, sems, *, tm):
    i = pl.program_id(0)
    e = te_ref[i]
    slot = sl_ref[i]
    new_expert = jnp.logical_or(i == 0, e != te_ref[jnp.maximum(i - 1, 0)])

    def fetch(expert, s):
        return [pltpu.make_async_copy(w.at[expert], wbuf.at[s, k], sems.at[s, k])
                for k, w in enumerate((wg_hbm, wu_hbm, wd_hbm))]

    @pl.when(i == 0)
    def _():
        for cp in fetch(e, slot):
            cp.start()

    @pl.when(new_expert)
    def _():
        for cp in fetch(e, slot):
            cp.wait()

        @pl.when(nx_ref[i] >= 0)
        def _():
            for cp in fetch(nx_ref[i], 1 - slot):
                cp.start()
        wg_bf[...] = wbuf[slot, 0].astype(BF16)
        wu_bf[...] = wbuf[slot, 1].astype(BF16)
        wd_bf[...] = wbuf[slot, 2].astype(BF16)

    @pl.when(tv_ref[i] > 0)
    def _():
        x = _load_token_tiles(xs_ref, tm).astype(BF16)
        g = jnp.dot(x, wg_bf[...], preferred_element_type=F32) + bg_ref[...]
        u = jnp.dot(x, wu_bf[...], preferred_element_type=F32) + bu_ref[...]
        g = jnp.minimum(g, SWIGLU_LIMIT)
        u = jnp.clip(u, -SWIGLU_LIMIT, SWIGLU_LIMIT)
        act = (u + 1.0) * g * (1.0 / (1.0 + jnp.exp(-SWIGLU_ALPHA * g)))
        y = jnp.dot(act.astype(BF16), wd_bf[...], preferred_element_type=F32) + bd_ref[...]
        _store_token_tiles(ys_ref, y)

    @pl.when(tv_ref[i] == 0)
    def _():
        ys_ref[...] = jnp.zeros_like(ys_ref)


def _experts(xs, plan, wg, bg, wu, bu, wd, bd, tm=MOE_TM):
    n_tiles = xs.shape[0] // (tm * TOK_ROWS)
    d, f = wg.shape[-2:]
    assert d == f == D_MODEL
    b_spec = lambda n: pl.BlockSpec((None, 1, n), lambda i, te, *_: (te[i], 0, 0))
    rows = pl.BlockSpec((tm * TOK_ROWS, LANES), lambda i, *_: (i, 0))
    hbm = pl.BlockSpec(memory_space=pl.ANY)
    return pl.pallas_call(
        functools.partial(_expert_kernel, tm=tm),
        grid_spec=pltpu.PrefetchScalarGridSpec(
            num_scalar_prefetch=4,
            grid=(n_tiles,),
            in_specs=[rows, hbm, b_spec(f), hbm, b_spec(f), hbm, b_spec(d)],
            out_specs=rows,
            scratch_shapes=[pltpu.VMEM((2, 3, d, f), F32),
                            pltpu.VMEM((d, f), BF16), pltpu.VMEM((d, f), BF16), pltpu.VMEM((f, d), BF16),
                            pltpu.SemaphoreType.DMA((2, 3))],
        ),
        out_shape=jax.ShapeDtypeStruct(xs.shape, F32),
        compiler_params=_cparams(1),
        name="moe_experts",
    )(*plan, xs, wg, bg.reshape(N_EXPERTS, 1, f), wu, bu.reshape(N_EXPERTS, 1, f),
      wd, bd.reshape(N_EXPERTS, 1, d))


def _combine_kernel(dest_ref, ys_ref, x1_ref, gate_ref, gt2_ref, gfin_ref, o_ref, buf, sem, *, tm):
    def make(n, t, d):
        return pltpu.make_async_copy(_tok(ys_ref, d), _tok(buf, t * tm + n), sem)
    _row_copies(dest_ref, tm, make)
    gate = gate_ref[...]
    y = gate[:, 0:1] * _load_token_tiles(buf, tm)
    for t in range(1, TOP_K):
        y = y + gate[:, t:t + 1] * _load_token_tiles(buf, tm, base=t * tm * TOK_ROWS)
    x2 = x1_ref[...] + gt2_ref[...] * y
    o_ref[...] = _rmsnorm(x2, gfin_ref[...])


def _combine(ys, dest, x1, gate, gt2, gfin, tm=TOK_TM):
    d = D_MODEL
    if x1.ndim == 3:
        b, s, _ = x1.shape
        tm = min(tm, s)
        nj = s // tm
        grid = (b, nj)
        row = lambda w: pl.BlockSpec((None, tm, w), lambda bi, j: (bi, j, 0))
        mod = pl.BlockSpec((None, 1, d), lambda bi, j: (bi, 0, 0))
        const = pl.BlockSpec((1, d), lambda bi, j: (0, 0))
        dspec = pl.BlockSpec((None, 1, tm * TOP_K), lambda bi, j: (bi * nj + j, 0, 0),
                             memory_space=pltpu.SMEM)
        n = b * s
    else:
        n = x1.shape[0]
        tm = min(tm, n)
        grid = (n // tm,)
        row = lambda w: pl.BlockSpec((tm, w), lambda i: (i, 0))
        mod = row(d)
        const = pl.BlockSpec((1, d), lambda i: (0, 0))
        dspec = pl.BlockSpec((None, 1, tm * TOP_K), lambda i: (i, 0, 0), memory_space=pltpu.SMEM)
    dest3 = dest.reshape(n // tm, 1, tm * TOP_K)
    return pl.pallas_call(
        functools.partial(_combine_kernel, tm=tm),
        grid=grid,
        in_specs=[dspec, pl.BlockSpec(memory_space=pl.ANY), row(d), row(LANES), mod, const],
        out_specs=row(d),
        out_shape=jax.ShapeDtypeStruct(x1.shape, F32),
        scratch_shapes=[pltpu.VMEM((TOP_K * tm * TOK_ROWS, LANES), F32), pltpu.SemaphoreType.DMA(())],
        compiler_params=_cparams(len(grid)),
        name="moe_combine",
    )(dest3, ys, x1, gate, gt2, gfin)


def _routing_plan(counts, n_pairs, tm=MOE_TM):
    pc = (counts + tm - 1) // tm * tm
    pend = jnp.cumsum(pc)
    pstart = pend - pc
    n_rows = -(-(n_pairs + N_EXPERTS * (tm - 1)) // tm) * tm
    n_tiles = n_rows // tm
    tile_row = jnp.arange(n_tiles, dtype=I32) * tm
    last_used = jnp.max(jnp.where(pc > 0, jnp.arange(N_EXPERTS, dtype=I32), 0))
    tile_e = jnp.minimum(jnp.sum((tile_row[:, None] >= pend[None, :]).astype(I32), axis=1), last_used)
    tile_valid = (tile_row < pend[-1]).astype(I32)
    ids = jnp.arange(N_EXPERTS, dtype=I32)
    used = pc > 0
    slot_e = (jnp.cumsum(used.astype(I32)) - 1) & 1
    later = jnp.where(used[None, :] & (ids[None, :] > ids[:, None]), ids[None, :], N_EXPERTS)
    next_e = jnp.min(later, axis=1)
    next_e = jnp.where(next_e == N_EXPERTS, -1, next_e)
    pick = (tile_e[:, None] == ids[None, :]).astype(I32)
    plan = (tile_e, tile_valid, jnp.sum(pick * next_e[None, :], axis=1),
            jnp.sum(pick * slot_e[None, :], axis=1))
    pad = ((pstart + counts).astype(I32), (pc - counts).astype(I32),
           (pend[-1:] // tm).astype(I32))
    return pstart.astype(I32), plan, n_rows, pad


def _rope_tables(pos):
    half = HEAD_DIM // 2
    inv = ROPE_THETA ** (-jnp.arange(half, dtype=F32) / half)
    ang = pos.astype(F32)[:, None] * inv[None, :]
    cos = jnp.cos(ang)
    sin = jnp.sin(ang)
    return jnp.concatenate([cos, cos, cos, cos], axis=1), jnp.concatenate([-sin, sin, -sin, sin], axis=1)


def kernel(x_prompt, x_sample, cache_k, cache_v, state_conv, c_prompt, c_sample, w_ada, b_ada,
           g_norm_mix, w_in, conv_w, g_attn_out, g_conv_out, w_out, g_norm_ffn, w_router, b_router,
           w_gate, b_gate, w_up, b_up, w_down, b_down, g_final):
    depth = w_in.shape[0]
    assert depth == 1, "single-layer trunk"
    bp, sp, d = x_prompt.shape
    bs, ts, _ = x_sample.shape
    ns = bs * ts
    n_cache = cache_k.shape[2]
    l = 0

    mods = _modulations(jnp.concatenate([c_prompt, c_sample], axis=0), w_ada[l], b_ada[l])
    mp = [m.reshape(bp, 1, d) for m in jnp.split(mods[:bp], 6, axis=-1)]
    ms = [jnp.repeat(m, ts, axis=0) for m in jnp.split(mods[bp:], 6, axis=-1)]

    w_in_bf = w_in[l].astype(BF16)
    w_out_bf = w_out[l].astype(BF16)
    wr_pad = jnp.pad(w_router[l], ((0, 0), (0, LANES - N_EXPERTS)))
    br_pad = jnp.pad(b_router[l].reshape(1, N_EXPERTS), ((0, 0), (0, LANES - N_EXPERTS)),
                     constant_values=NEG)
    g_mix = g_norm_mix[l].reshape(1, d)
    g_ffn = g_norm_ffn[l].reshape(1, d)
    g_att = g_attn_out[l].reshape(1, ATT_W)
    g_cnv = g_conv_out[l].reshape(1, CONV_W)
    g_fin = g_final.reshape(1, d)

    cos_p, sin_p = _rope_tables(jnp.arange(sp))
    q_p, k_p, v_p, conv_p, tail_p = _inproj_prompt(x_prompt, mp[0], mp[1], g_mix, w_in_bf,
                                                   cos_p, sin_p, conv_w[l])
    att_p = _attn_prompt(q_p, k_p, v_p)

    xs_rows = x_sample.reshape(ns, d)
    cos_s, sin_s = _rope_tables(PAST_LEN + jnp.arange(ts))
    cos_s = jnp.tile(cos_s, (bs, 1))
    sin_s = jnp.tile(sin_s, (bs, 1))
    st = state_conv[l]
    zrow = jnp.zeros((bs, 1, CONV_W), F32)
    s1 = jnp.concatenate([st[:, 1:2], zrow, zrow, zrow], axis=1).reshape(ns, CONV_W)
    s2 = jnp.concatenate([st[:, 0:1], st[:, 1:2], zrow, zrow], axis=1).reshape(ns, CONV_W)
    q_s, k_s, v_s, conv_s, cu_s = _inproj_sample(xs_rows, ms[0], ms[1], g_mix, w_in_bf,
                                                 cos_s, sin_s, conv_w[l], s1, s2, ts)
    to_t = lambda a: jnp.pad(jnp.transpose(a.reshape(bs, ts, N_HEADS, HEAD_DIM), (0, 2, 3, 1)),
                             ((0, 0), (0, 0), (0, 0), (0, LANES - ts)))
    q8 = jnp.pad(q_s.reshape(bs, ts, ATT_W), ((0, 0), (0, 8 - ts), (0, 0)))
    kt = jnp.transpose(cache_k[l], (0, 2, 3, 1))
    vt = jnp.transpose(cache_v[l], (0, 2, 3, 1))
    att_s = _attn_sample(q8, kt, vt, to_t(k_s), to_t(v_s), ts)[:, :ts].reshape(ns, ATT_W)

    cnt0 = jnp.zeros((8, LANES), F32)
    x1_p, h2_p, idx_p, gate_p, rank_p, cnt_p = _merge(
        att_p, conv_p, x_prompt, mp[2], mp[3], mp[4], g_att, g_cnv, g_ffn,
        w_out_bf, wr_pad, br_pad, cnt0, TOK_TM)
    x1_s, h2_s, idx_s, gate_s, rank_s, cnt_all = _merge(
        att_s, conv_s, xs_rows, ms[2], ms[3], ms[4], g_att, g_cnv, g_ffn,
        w_out_bf, wr_pad, br_pad, cnt_p, ns)

    n_tok = bp * sp + ns
    counts = cnt_all[0, :N_EXPERTS].astype(I32)
    pstart, plan, n_rows, pad = _routing_plan(counts, n_tok * TOP_K)
    ids = jnp.arange(N_EXPERTS, dtype=I32)
    slot_of = lambda idx, rank: jnp.sum(
        jnp.where(idx[..., :TOP_K, None] == ids, pstart, 0), axis=-1) + rank[..., :TOP_K]
    dest_p = slot_of(idx_p, rank_p).reshape(bp * sp, TOP_K)
    dest_s = slot_of(idx_s, rank_s)
    xs_sorted = _dispatch(h2_p.reshape(bp * sp * TOK_ROWS, LANES), dest_p, h2_s, dest_s, pad, n_rows)
    ys = _experts(xs_sorted, plan, w_gate[l], b_gate[l], w_up[l], b_up[l], w_down[l], b_down[l])
    y_prompt = _combine(ys, dest_p, x1_p, gate_p, mp[5], g_fin)
    y_sample = _combine(ys, dest_s, x1_s, gate_s, ms[5], g_fin).reshape(bs, ts, d)

    heads = lambda a, b, s: a.reshape(1, b, s, N_HEADS, HEAD_DIM)
    keep = min(WINDOW_MAX, sp)
    return (y_prompt, y_sample,
            heads(k_p, bp, sp)[:, :, sp - keep:], heads(v_p, bp, sp)[:, :, sp - keep:],
            tail_p[:, 8 - (CONV_K - 1):][None],
            heads(k_s, bs, ts), heads(v_s, bs, ts),
            cu_s.reshape(bs, ts, CONV_W)[:, ts - (CONV_K - 1):][None])
```

```python
import functools

import jax
import jax.numpy as jnp
import numpy as np
from jax import lax
from jax.experimental import pallas as pl
from jax.experimental.pallas import tpu as pltpu

F32 = jnp.float32
BF16 = jnp.bfloat16
I32 = jnp.int32
HIGHEST = lax.Precision.HIGHEST

D_MODEL = 1024
HEAD_DIM = 64
N_HEADS = 12
ATT_W = N_HEADS * HEAD_DIM
CONV_W = D_MODEL - ATT_W
CONV_K = 3
PATTERNS = ((128, 1), (512, 4), (2048, 16))
WINDOW_MAX = 2048
PAST_LEN = 16384
ROPE_THETA = 10000.0
N_EXPERTS = 32
TOP_K = 4
SWIGLU_ALPHA = 1.702
SWIGLU_LIMIT = 7.0
NORM_EPS = 1e-6
NEG = -1e30
IN_W = 3 * ATT_W + 3 * CONV_W

LANES = 128
Q_BLK = 128
NEAR_W = 512
NEAR_KEYS = NEAR_W + Q_BLK
FAR_D = 16
MOE_TM = 256
TOK_TM = 256
VMEM_LIMIT = 56 * 1024 * 1024
TOK_ROWS = D_MODEL // LANES


def _cparams(n_axes, vmem=VMEM_LIMIT):
    return pltpu.CompilerParams(dimension_semantics=("arbitrary",) * n_axes,
                                vmem_limit_bytes=vmem)


def _multiplicity(delta):
    delta = np.asarray(delta)
    c = np.zeros(delta.shape, np.float32)
    for w, d in PATTERNS:
        c += ((delta >= 0) & (delta <= w) & (delta % d == 0)).astype(np.float32)
    return c


def _ada_kernel(c_ref, w_ref, b_ref, o_ref):
    c = c_ref[...]
    s = c / (1.0 + jnp.exp(-c))
    o_ref[...] = jnp.dot(s, w_ref[...], precision=HIGHEST,
                         preferred_element_type=F32) + b_ref[...]


def _modulations(c_all, w_ada, b_ada):
    r, d = c_all.shape
    n = w_ada.shape[1]
    tn = 1536
    return pl.pallas_call(
        _ada_kernel,
        grid=(n // tn,),
        in_specs=[pl.BlockSpec((r, d), lambda j: (0, 0)),
                  pl.BlockSpec((d, tn), lambda j: (0, j)),
                  pl.BlockSpec((1, tn), lambda j: (0, j))],
        out_specs=pl.BlockSpec((r, tn), lambda j: (0, j)),
        out_shape=jax.ShapeDtypeStruct((r, n), F32),
        compiler_params=_cparams(1),
        name="ada_modulation",
    )(c_all, w_ada, b_ada.reshape(1, n))


def _norm_mod(x, g, shift, scale):
    ms = jnp.mean(x * x, axis=-1, keepdims=True)
    return (x * lax.rsqrt(ms + NORM_EPS) * g) * (1.0 + scale) + shift


def _rmsnorm(x, g):
    ms = jnp.mean(x * x, axis=-1, keepdims=True)
    return x * lax.rsqrt(ms + NORM_EPS) * g


def _store_token_tiles(ref, x, base=0):
    tm = x.shape[0]
    for c in range(TOK_ROWS):
        ref[pl.ds(base + c, tm, stride=TOK_ROWS), :] = x[:, c * LANES:(c + 1) * LANES]


def _load_token_tiles(ref, tm, base=0):
    return jnp.concatenate([ref[pl.ds(base + c, tm, stride=TOK_ROWS), :] for c in range(TOK_ROWS)],
                           axis=1)


def _swap_halves(xc):
    lane = lax.broadcasted_iota(I32, xc.shape, 1)
    first = (lane & (HEAD_DIM - 1)) < HEAD_DIM // 2
    return jnp.where(first, pltpu.roll(xc, LANES - HEAD_DIM // 2, 1),
                     pltpu.roll(xc, HEAD_DIM // 2, 1))


def _rope(x, cosf, sinf):
    outs = []
    for c in range(x.shape[1] // LANES):
        xc = x[:, c * LANES:(c + 1) * LANES]
        outs.append(xc * cosf + _swap_halves(xc) * sinf)
    return jnp.concatenate(outs, axis=1)


def _inproj_kernel(*refs, tm, sample, seq_per_batch):
    if sample:
        (x_ref, sh_ref, sc_ref, g_ref, w_ref, cos_ref, sin_ref, cw_ref, s1_ref, s2_ref,
         q_ref, k_ref, v_ref, conv_ref, cu_ref, cu_ext) = refs
    else:
        (x_ref, sh_ref, sc_ref, g_ref, w_ref, cos_ref, sin_ref, cw_ref,
         q_ref, k_ref, v_ref, conv_ref, tail_ref, cu_ext) = refs
    h = _norm_mod(x_ref[...], g_ref[...], sh_ref[...], sc_ref[...])
    z = jnp.dot(h.astype(BF16), w_ref[...], preferred_element_type=F32)
    cosf = cos_ref[...]
    sinf = sin_ref[...]
    q_ref[...] = _rope(z[:, 0:ATT_W], cosf, sinf) * (HEAD_DIM ** -0.5)
    k_ref[...] = _rope(z[:, ATT_W:2 * ATT_W], cosf, sinf)
    v_ref[...] = z[:, 2 * ATT_W:3 * ATT_W]
    o = 3 * ATT_W
    gb = z[:, o:o + CONV_W]
    cu = z[:, o + CONV_W:o + 2 * CONV_W] * z[:, o + 2 * CONV_W:o + 3 * CONV_W]
    if sample:
        cu_ext[0:8, :] = jnp.zeros((8, CONV_W), F32)
    else:
        @pl.when(pl.program_id(1) == 0)
        def _():
            cu_ext[0:8, :] = jnp.zeros((8, CONV_W), F32)
    cu_ext[8:8 + tm, :] = cu
    p1 = cu_ext[7:7 + tm, :]
    p2 = cu_ext[6:6 + tm, :]
    if sample:
        t = lax.broadcasted_iota(I32, (tm, CONV_W), 0) % seq_per_batch
        p1 = jnp.where(t >= 1, p1, 0.0) + s1_ref[...]
        p2 = jnp.where(t >= 2, p2, 0.0) + s2_ref[...]
        cu_ref[...] = cu
    cw = cw_ref[...]
    conv_ref[...] = gb * (cw[0:1, :] * p2 + cw[1:2, :] * p1 + cw[2:3, :] * cu)
    if not sample:
        tail = cu_ext[tm:tm + 8, :]
        tail_ref[...] = tail
        cu_ext[0:8, :] = tail


def _inproj_prompt(x, shift, scale, g, w_bf, cosf, sinf, conv_w, tm=512):
    b, s, d = x.shape
    row = lambda bi, j: (bi, j, 0)
    per_b = lambda bi, j: (bi, 0, 0)
    const = lambda bi, j: (0, 0)
    outs = pl.pallas_call(
        functools.partial(_inproj_kernel, tm=tm, sample=False, seq_per_batch=s),
        grid=(b, s // tm),
        in_specs=[pl.BlockSpec((None, tm, d), row),
                  pl.BlockSpec((None, 1, d), per_b),
                  pl.BlockSpec((None, 1, d), per_b),
                  pl.BlockSpec((1, d), const),
                  pl.BlockSpec((d, IN_W), const),
                  pl.BlockSpec((tm, LANES), lambda bi, j: (j, 0)),
                  pl.BlockSpec((tm, LANES), lambda bi, j: (j, 0)),
                  pl.BlockSpec((CONV_K, CONV_W), const)],
        out_specs=[pl.BlockSpec((None, tm, ATT_W), row),
                   pl.BlockSpec((None, tm, ATT_W), row),
                   pl.BlockSpec((None, tm, ATT_W), row),
                   pl.BlockSpec((None, tm, CONV_W), row),
                   pl.BlockSpec((None, 8, CONV_W), per_b)],
        out_shape=[jax.ShapeDtypeStruct((b, s, ATT_W), F32),
                   jax.ShapeDtypeStruct((b, s, ATT_W), F32),
                   jax.ShapeDtypeStruct((b, s, ATT_W), F32),
                   jax.ShapeDtypeStruct((b, s, CONV_W), F32),
                   jax.ShapeDtypeStruct((b, 8, CONV_W), F32)],
        scratch_shapes=[pltpu.VMEM((tm + 8, CONV_W), F32)],
        compiler_params=_cparams(2),
        name="inproj_prompt",
    )(x, shift, scale, g, w_bf, cosf, sinf, conv_w)
    return outs


def _inproj_sample(x, shift, scale, g, w_bf, cosf, sinf, conv_w, s1, s2, seq_per_batch):
    n, d = x.shape
    full = lambda shape: pl.BlockSpec(shape, lambda i: (0, 0))
    outs = pl.pallas_call(
        functools.partial(_inproj_kernel, tm=n, sample=True, seq_per_batch=seq_per_batch),
        grid=(1,),
        in_specs=[full((n, d)), full((n, d)), full((n, d)), full((1, d)), full((d, IN_W)),
                  full((n, LANES)), full((n, LANES)), full((CONV_K, CONV_W)),
                  full((n, CONV_W)), full((n, CONV_W))],
        out_specs=[full((n, ATT_W)), full((n, ATT_W)), full((n, ATT_W)),
                   full((n, CONV_W)), full((n, CONV_W))],
        out_shape=[jax.ShapeDtypeStruct((n, ATT_W), F32)] * 3
        + [jax.ShapeDtypeStruct((n, CONV_W), F32)] * 2,
        scratch_shapes=[pltpu.VMEM((n + 8, CONV_W), F32)],
        compiler_params=_cparams(1),
        name="inproj_sample",
    )(x, shift, scale, g, w_bf, cosf, sinf, conv_w, s1, s2)
    return outs


def _stack_heads(x):
    lo = lax.broadcasted_iota(I32, x.shape, 1) < HEAD_DIM
    zero = jnp.zeros_like(x)
    return jnp.concatenate([jnp.where(lo, x, zero), jnp.where(lo, zero, x)], axis=0)


def _split_pv(r0, r1, m, rows):
    lo = lax.broadcasted_iota(I32, r0.shape, 1) < HEAD_DIM
    num = jnp.where(lo, r0, r1)
    den = jnp.where(lo, pltpu.roll(r0, HEAD_DIM, 1), pltpu.roll(r1, HEAD_DIM, 1))
    mx = jnp.where(lo, jnp.broadcast_to(m[:rows], r0.shape), jnp.broadcast_to(m[rows:], r0.shape))
    return num, den, mx


def _attn_prompt_kernel(q_ref, k_ref, v_ref, lmult_ref, o_ref,
                        kpad, v0pad, v1pad, num3, den3, max3,
                        *, seq, far_unroll, near_unroll):
    nt = (((1,), (1,)), ((), ()))
    n_blk = seq // Q_BLK
    lo1 = lax.broadcasted_iota(I32, (Q_BLK, LANES), 1) < HEAD_DIM
    ones = jnp.ones((Q_BLK, LANES), F32)

    zpad = jnp.zeros((NEAR_W, LANES), BF16)
    kpad[0:NEAR_W, :] = zpad
    v0pad[0:NEAR_W, :] = zpad
    v1pad[0:NEAR_W, :] = zpad

    def fill(i, c):
        s0 = pl.multiple_of(i * Q_BLK, Q_BLK)
        d0 = pl.multiple_of(i * Q_BLK + NEAR_W, Q_BLK)
        kpad[pl.ds(d0, Q_BLK), :] = k_ref[pl.ds(s0, Q_BLK), :].astype(BF16)
        vb = v_ref[pl.ds(s0, Q_BLK), :]
        v0pad[pl.ds(d0, Q_BLK), :] = jnp.where(lo1, vb, ones).astype(BF16)
        v1pad[pl.ds(d0, Q_BLK), :] = jnp.where(lo1, ones, vb).astype(BF16)
        return c
    lax.fori_loop(0, n_blk, fill, 0)

    row = lax.broadcasted_iota(I32, (2 * Q_BLK, Q_BLK), 0) & (Q_BLK - 1)
    col = lax.broadcasted_iota(I32, (2 * Q_BLK, Q_BLK), 1)
    causal = col <= row

    def far(r, c):
        sl = pl.ds(r, seq // FAR_D, stride=FAR_D)
        q2 = _stack_heads(q_ref[sl, :]).astype(BF16)
        kr = k_ref[sl, :].astype(BF16)
        vr = v_ref[sl, :]
        s = lax.dot_general(q2, kr, nt, preferred_element_type=F32)
        s = jnp.where(causal, s, NEG)
        m = jnp.max(s, axis=1, keepdims=True)
        p = jnp.exp(s - m).astype(BF16)
        r0 = jnp.dot(p[:Q_BLK], jnp.where(lo1, vr, ones).astype(BF16), preferred_element_type=F32)
        r1 = jnp.dot(p[Q_BLK:], jnp.where(lo1, ones, vr).astype(BF16), preferred_element_type=F32)
        num, den, mx = _split_pv(r0, r1, m, Q_BLK)
        num3[sl, :] = num
        den3[sl, :] = den
        max3[sl, :] = mx
        return c
    lax.fori_loop(0, FAR_D, far, 0, unroll=far_unroll)

    kcol = lax.broadcasted_iota(I32, (1, NEAR_KEYS), 1)

    def near(first_blocks):
        def body(i, c):
            s0 = pl.multiple_of(i * Q_BLK, Q_BLK)
            q2 = _stack_heads(q_ref[pl.ds(s0, Q_BLK), :]).astype(BF16)
            kw = kpad[pl.ds(s0, NEAR_KEYS), :]
            s = lax.dot_general(q2, kw, nt, preferred_element_type=F32) + lmult_ref[...]
            if first_blocks:
                s = jnp.where(kcol >= NEAR_W - s0, s, NEG)
            m = jnp.max(s, axis=1, keepdims=True)
            p = jnp.exp(s - m).astype(BF16)
            r0 = jnp.dot(p[:Q_BLK], v0pad[pl.ds(s0, NEAR_KEYS), :], preferred_element_type=F32)
            r1 = jnp.dot(p[Q_BLK:], v1pad[pl.ds(s0, NEAR_KEYS), :], preferred_element_type=F32)
            num, den, mx = _split_pv(r0, r1, m, Q_BLK)
            mx3 = max3[pl.ds(s0, Q_BLK), :]
            mm = jnp.maximum(mx, mx3)
            a = jnp.exp(mx - mm)
            b = jnp.exp(mx3 - mm)
            o_ref[pl.ds(s0, Q_BLK), :] = ((num * a + num3[pl.ds(s0, Q_BLK), :] * b)
                                          / (den * a + den3[pl.ds(s0, Q_BLK), :] * b))
            return c
        return body

    n_first = NEAR_W // Q_BLK
    lax.fori_loop(0, n_first, near(True), 0, unroll=near_unroll)
    lax.fori_loop(n_first, n_blk, near(False), 0, unroll=near_unroll)


def _near_table():
    i = np.arange(Q_BLK)[:, None]
    kl = np.arange(NEAR_KEYS)[None, :]
    delta = i + NEAR_W - kl
    mult = np.zeros(delta.shape, np.float32)
    for w, d in PATTERNS[:2]:
        mult += ((delta >= 0) & (delta <= w) & (delta % d == 0)).astype(np.float32)
    lmult = np.where(mult > 0, np.log(np.maximum(mult, 1.0)), NEG).astype(np.float32)
    return np.tile(lmult, (2, 1))


def _attn_prompt(q, k, v, far_unroll=4, near_unroll=4):
    b, s, _ = q.shape
    lmult = _near_table()
    blk = pl.BlockSpec((None, s, LANES), lambda bi, hp: (bi, 0, hp))
    tab = pl.BlockSpec((2 * Q_BLK, NEAR_KEYS), lambda bi, hp: (0, 0))
    return pl.pallas_call(
        functools.partial(_attn_prompt_kernel, seq=s, far_unroll=far_unroll, near_unroll=near_unroll),
        grid=(b, ATT_W // LANES),
        in_specs=[blk, blk, blk, tab],
        out_specs=blk,
        out_shape=jax.ShapeDtypeStruct((b, s, ATT_W), F32),
        scratch_shapes=[pltpu.VMEM((s + NEAR_W, LANES), BF16)] * 3
        + [pltpu.VMEM((s, LANES), F32)] * 3,
        compiler_params=_cparams(2),
        name="attn_prompt",
    )(q, k, v, jnp.asarray(lmult))


def _attn_sample_kernel(q_ref, kt_ref, vt_ref, knt_ref, vnt_ref, bias_ref, mult_ref, o_ref):
    nt = (((1,), (1,)), ((), ()))
    q = q_ref[...]
    bias = bias_ref[...]
    mult = mult_ref[...]
    n_c = kt_ref.shape[-1]
    for h in range(N_HEADS):
        qh = q[:, h * HEAD_DIM:(h + 1) * HEAD_DIM].astype(BF16)
        s = jnp.concatenate(
            [jnp.dot(qh, kt_ref[h].astype(BF16), preferred_element_type=F32),
             jnp.dot(qh, knt_ref[h].astype(BF16), preferred_element_type=F32)], axis=1) + bias
        m = jnp.max(s, axis=1, keepdims=True)
        p = jnp.exp(s - m) * mult
        den = jnp.sum(p, axis=1, keepdims=True)
        pb = p.astype(BF16)
        num = (lax.dot_general(pb[:, :n_c], vt_ref[h].astype(BF16), nt, preferred_element_type=F32)
               + lax.dot_general(pb[:, n_c:], vnt_ref[h].astype(BF16), nt, preferred_element_type=F32))
        o_ref[:, h * HEAD_DIM:(h + 1) * HEAD_DIM] = num / den


def _sample_tables(n_cache, t_new):
    t = np.arange(8)[:, None] % t_new
    rho = np.arange(n_cache)[None, :]
    c_cache = _multiplicity(t + n_cache - rho)
    tp = np.arange(LANES)[None, :]
    c_new = np.where(tp < t_new, _multiplicity(t - tp), 0.0)
    mult = np.concatenate([c_cache, c_new], axis=1).astype(np.float32)
    bias = np.where(mult > 0, 0.0, NEG).astype(np.float32)
    return bias, mult


def _attn_sample(q8, kt, vt, knt, vnt, t_new):
    b = q8.shape[0]
    n_c = kt.shape[-1]
    bias, mult = _sample_tables(n_c, t_new)
    cache = pl.BlockSpec((None, N_HEADS, HEAD_DIM, n_c), lambda i: (i, 0, 0, 0))
    new = pl.BlockSpec((None, N_HEADS, HEAD_DIM, LANES), lambda i: (i, 0, 0, 0))
    row = pl.BlockSpec((None, 8, ATT_W), lambda i: (i, 0, 0))
    tab = pl.BlockSpec((8, n_c + LANES), lambda i: (0, 0))
    return pl.pallas_call(
        _attn_sample_kernel,
        grid=(b,),
        in_specs=[row, cache, cache, new, new, tab, tab],
        out_specs=row,
        out_shape=jax.ShapeDtypeStruct((b, 8, ATT_W), F32),
        compiler_params=_cparams(1),
        name="attn_sample",
    )(q8, kt, vt, knt, vnt, jnp.asarray(bias), jnp.asarray(mult))


def _merge_kernel(att_ref, conv_ref, x_ref, gt1_ref, sh2_ref, sc2_ref, ga_ref, gc_ref, gf_ref,
                  wo_ref, wr_ref, wrl_ref, br_ref, cnt0_ref, tri_ref,
                  x1_ref, h2_ref, idx_ref, gate_ref, rank_ref, cnt_ref, cnt_sc, *, n_axes):
    first = pl.program_id(0) == 0
    if n_axes == 2:
        first = jnp.logical_and(first, pl.program_id(1) == 0)

    @pl.when(first)
    def _():
        cnt_sc[...] = cnt0_ref[...]

    an = _rmsnorm(att_ref[...], ga_ref[...]).astype(BF16)
    cn = _rmsnorm(conv_ref[...], gc_ref[...]).astype(BF16)
    mix = (jnp.dot(an, wo_ref[0:ATT_W, :], preferred_element_type=F32)
           + jnp.dot(cn, wo_ref[ATT_W:D_MODEL, :], preferred_element_type=F32))
    x1 = x_ref[...] + gt1_ref[...] * mix
    x1_ref[...] = x1
    h2 = _norm_mod(x1, gf_ref[...], sh2_ref[...], sc2_ref[...])
    _store_token_tiles(h2_ref, h2)
    hi = h2.astype(BF16)
    lo = (h2 - hi.astype(F32)).astype(BF16)
    logits = (jnp.dot(hi, wr_ref[...], preferred_element_type=F32)
              + (jnp.dot(hi, wrl_ref[...], preferred_element_type=F32)
                 + jnp.dot(lo, wr_ref[...], preferred_element_type=F32))) + br_ref[...]
    tm = logits.shape[0]
    lane = lax.broadcasted_iota(I32, (tm, LANES), 1)
    work = logits
    vals, idxs = [], []
    for _ in range(TOP_K):
        mx = jnp.max(work, axis=1, keepdims=True)
        ix = jnp.min(jnp.where(work == mx, lane, LANES), axis=1, keepdims=True)
        vals.append(mx)
        idxs.append(ix)
        work = jnp.where(lane == ix, 3.0 * NEG, work)
    es = [jnp.exp(v - vals[0]) for v in vals]
    den = es[0] + es[1] + es[2] + es[3]
    onehot = jnp.zeros((tm, LANES), F32)
    for ix in idxs:
        onehot = onehot + (lane == ix).astype(F32)
    before = jnp.dot(tri_ref[...], onehot.astype(BF16), preferred_element_type=F32) + cnt_sc[0:1, :]
    idx_o = jnp.zeros((tm, LANES), I32)
    gate_o = jnp.zeros((tm, LANES), F32)
    rank_o = jnp.zeros((tm, LANES), F32)
    for t in range(TOP_K):
        rk = jnp.sum(jnp.where(lane == idxs[t], before, 0.0), axis=1, keepdims=True)
        idx_o = jnp.where(lane == t, idxs[t], idx_o)
        gate_o = jnp.where(lane == t, es[t] / den, gate_o)
        rank_o = jnp.where(lane == t, rk, rank_o)
    idx_ref[...] = idx_o
    gate_ref[...] = gate_o
    rank_ref[...] = rank_o.astype(I32)
    cnt_sc[...] = cnt_sc[...] + jnp.sum(onehot, axis=0, keepdims=True)
    cnt_ref[...] = cnt_sc[...]


def _merge(att, conv, x, gt1, sh2, sc2, ga, gc, gf, wo_bf, wr_pad, br_pad, cnt0, tm):
    d = D_MODEL
    wr_hi = wr_pad.astype(BF16)
    wr_lo = (wr_pad - wr_hi.astype(F32)).astype(BF16)
    tri = jnp.asarray(np.tril(np.ones((tm, tm), np.float32), -1), BF16)
    if att.ndim == 3:
        b, s, _ = att.shape
        grid = (b, s // tm)
        row = lambda w: pl.BlockSpec((None, tm, w), lambda bi, j: (bi, j, 0))
        mod = pl.BlockSpec((None, 1, d), lambda bi, j: (bi, 0, 0))
        const = lambda shape: pl.BlockSpec(shape, lambda bi, j: (0, 0))
        lead = (b, s)
        tiles = pl.BlockSpec((None, tm * TOK_ROWS, LANES), lambda bi, j: (bi, j, 0))
        tiles_shape = (b, s * TOK_ROWS, LANES)
    else:
        n = att.shape[0]
        grid = (n // tm,)
        row = lambda w: pl.BlockSpec((tm, w), lambda i: (i, 0))
        mod = row(d)
        const = lambda shape: pl.BlockSpec(shape, lambda i: (0, 0))
        lead = (n,)
        tiles = pl.BlockSpec((tm * TOK_ROWS, LANES), lambda i: (i, 0))
        tiles_shape = (n * TOK_ROWS, LANES)
    return pl.pallas_call(
        functools.partial(_merge_kernel, n_axes=len(grid)),
        grid=grid,
        in_specs=[row(ATT_W), row(CONV_W), row(d), mod, mod, mod,
                  const((1, ATT_W)), const((1, CONV_W)), const((1, d)),
                  const((d, d)), const((d, LANES)), const((d, LANES)), const((1, LANES)),
                  const((8, LANES)), const((tm, tm))],
        out_specs=[row(d), tiles, row(LANES), row(LANES), row(LANES), const((8, LANES))],
        out_shape=[jax.ShapeDtypeStruct(lead + (d,), F32),
                   jax.ShapeDtypeStruct(tiles_shape, F32),
                   jax.ShapeDtypeStruct(lead + (LANES,), I32),
                   jax.ShapeDtypeStruct(lead + (LANES,), F32),
                   jax.ShapeDtypeStruct(lead + (LANES,), I32),
                   jax.ShapeDtypeStruct((8, LANES), F32)],
        scratch_shapes=[pltpu.VMEM((8, LANES), F32)],
        compiler_params=_cparams(len(grid)),
        name="merge_route",
    )(att, conv, x, gt1, sh2, sc2, ga, gc, gf, wo_bf, wr_hi, wr_lo, br_pad, cnt0, tri)


def _tok(ref, n, count=1):
    return ref.at[pl.ds(pl.multiple_of(n * TOK_ROWS, TOK_ROWS), count * TOK_ROWS)]


ROW_DMA_UNROLL = 4


def _start_rows(dest_ref, tm, make):
    def start(n, c):
        for t in range(TOP_K):
            make(n, t, dest_ref[0, n * TOP_K + t]).start(priority=t % 2)
        return c
    lax.fori_loop(0, tm, start, 0, unroll=ROW_DMA_UNROLL)


def _wait_rows(dest_ref, tm, make):
    def wait(n, c):
        for t in range(TOP_K):
            make(n, t, dest_ref[0, n * TOP_K + t]).wait()
        return c
    lax.fori_loop(0, tm, wait, 0, unroll=2 * ROW_DMA_UNROLL)


def _row_copies(dest_ref, tm, make):
    _start_rows(dest_ref, tm, make)
    _wait_rows(dest_ref, tm, make)


def _dispatch_kernel(ps_ref, pn_ref, t0_ref, dest_ref, h_ref, dest_s_ref, hs_ref, xs_ref,
                     zbuf, sem, *, tm, n_s, row_tm, n_tiles):
    def make(n, t, d):
        return pltpu.make_async_copy(_tok(h_ref, n), _tok(xs_ref, d), sem)
    _row_copies(dest_ref, tm, make)

    @pl.when(pl.program_id(0) == pl.num_programs(0) - 1)
    def _():
        def make_s(n, t, d):
            return pltpu.make_async_copy(_tok(hs_ref, n), _tok(xs_ref, d), sem)
        _row_copies(dest_s_ref, n_s, make_s)
        _zero_fill(ps_ref, pn_ref, t0_ref, xs_ref, zbuf, sem, row_tm, n_tiles)


def _dispatch(h2, dest, h2_s, dest_s, pad, n_rows, tm=2 * TOK_TM, row_tm=MOE_TM):
    n = h2.shape[0] // TOK_ROWS
    n_s = h2_s.shape[0] // TOK_ROWS
    nt = n // tm
    return pl.pallas_call(
        functools.partial(_dispatch_kernel, tm=tm, n_s=n_s, row_tm=row_tm, n_tiles=n_rows // row_tm),
        grid_spec=pltpu.PrefetchScalarGridSpec(
            num_scalar_prefetch=3,
            grid=(nt,),
            in_specs=[pl.BlockSpec((None, 1, tm * TOP_K), lambda i, *_: (i, 0, 0), memory_space=pltpu.SMEM),
                      pl.BlockSpec((tm * TOK_ROWS, LANES), lambda i, *_: (i, 0)),
                      pl.BlockSpec((1, n_s * TOP_K), lambda i, *_: (0, 0), memory_space=pltpu.SMEM),
                      pl.BlockSpec((n_s * TOK_ROWS, LANES), lambda i, *_: (0, 0))],
            out_specs=pl.BlockSpec(memory_space=pl.ANY),
            scratch_shapes=[pltpu.VMEM((row_tm * TOK_ROWS, LANES), F32), pltpu.SemaphoreType.DMA(())],
        ),
        out_shape=jax.ShapeDtypeStruct((n_rows * TOK_ROWS, LANES), F32),
        compiler_params=_cparams(1),
        name="moe_dispatch",
    )(*pad, dest.reshape(nt, 1, tm * TOP_K), h2, dest_s.reshape(1, n_s * TOP_K), h2_s)


def _zero_fill(ps_ref, pn_ref, t0_ref, xs_ref, zbuf, sem, tm, n_tiles):
    zbuf[...] = jnp.zeros_like(zbuf)
    bits = [tm >> (k + 1) for k in range(tm.bit_length() - 1)]

    def segments(act):
        def seg(e, c):
            off = ps_ref[e]
            ln = pn_ref[e]
            for bit in bits:
                @pl.when((ln & bit) != 0)
                def _(off=off, bit=bit):
                    act(pltpu.make_async_copy(_tok(zbuf, 0, bit), _tok(xs_ref, off, bit), sem))
                off = off + (ln & bit)
            return c
        lax.fori_loop(0, N_EXPERTS, seg, 0)

        def tile(i, c):
            act(pltpu.make_async_copy(zbuf, _tok(xs_ref, i * tm, tm), sem))
            return c
        lax.fori_loop(t0_ref[0], n_tiles, tile, 0)

    segments(lambda cp: cp.start())
    segments(lambda cp: cp.wait())


def _expert_kernel(te_ref, tv_ref, nx_ref, sl_ref, xs_ref, wg_hbm, bg_ref, wu_hbm, bu_ref,
                   wd_hbm, bd_ref, ys_ref, wbuf, wg_bf, wu_bf, wd_bf, sems, *, tm):
    i = pl.program_id(0)
    e = te_ref[i]
    slot = sl_ref[i]
    new_expert = jnp.logical_or(i == 0, e != te_ref[jnp.maximum(i - 1, 0)])

    def fetch(expert, s):
        return [pltpu.make_async_copy(w.at[expert], wbuf.at[s, k], sems.at[s, k])
                for k, w in enumerate((wg_hbm, wu_hbm, wd_hbm))]

    @pl.when(i == 0)
    def _():
        for cp in fetch(e, slot):
            cp.start()

    @pl.when(new_expert)
    def _():
        for cp in fetch(e, slot):
            cp.wait()

        @pl.when(nx_ref[i] >= 0)
        def _():
            for cp in fetch(nx_ref[i], 1 - slot):
                cp.start()
        wg_bf[...] = wbuf[slot, 0].astype(BF16)
        wu_bf[...] = wbuf[slot, 1].astype(BF16)
        wd_bf[...] = wbuf[slot, 2].astype(BF16)

    @pl.when(tv_ref[i] > 0)
    def _():
        x = _load_token_tiles(xs_ref, tm).astype(BF16)
        g = jnp.dot(x, wg_bf[...], preferred_element_type=F32) + bg_ref[...]
        u = jnp.dot(x, wu_bf[...], preferred_element_type=F32) + bu_ref[...]
        g = jnp.minimum(g, SWIGLU_LIMIT)
        u = jnp.clip(u, -SWIGLU_LIMIT, SWIGLU_LIMIT)
        act = (u + 1.0) * g * (1.0 / (1.0 + jnp.exp(-SWIGLU_ALPHA * g)))
        y = jnp.dot(act.astype(BF16), wd_bf[...], preferred_element_type=F32) + bd_ref[...]
        _store_token_tiles(ys_ref, y)

    @pl.when(tv_ref[i] == 0)
    def _():
        ys_ref[...] = jnp.zeros_like(ys_ref)


def _experts(xs, plan, wg, bg, wu, bu, wd, bd, tm=MOE_TM):
    n_tiles = xs.shape[0] // (tm * TOK_ROWS)
    d, f = wg.shape[-2:]
    assert d == f == D_MODEL
    b_spec = lambda n: pl.BlockSpec((None, 1, n), lambda i, te, *_: (te[i], 0, 0))
    rows = pl.BlockSpec((tm * TOK_ROWS, LANES), lambda i, *_: (i, 0))
    hbm = pl.BlockSpec(memory_space=pl.ANY)
    return pl.pallas_call(
        functools.partial(_expert_kernel, tm=tm),
        grid_spec=pltpu.PrefetchScalarGridSpec(
            num_scalar_prefetch=4,
            grid=(n_tiles,),
            in_specs=[rows, hbm, b_spec(f), hbm, b_spec(f), hbm, b_spec(d)],
            out_specs=rows,
            scratch_shapes=[pltpu.VMEM((2, 3, d, f), F32),
                            pltpu.VMEM((d, f), BF16), pltpu.VMEM((d, f), BF16), pltpu.VMEM((f, d), BF16),
                            pltpu.SemaphoreType.DMA((2, 3))],
        ),
        out_shape=jax.ShapeDtypeStruct(xs.shape, F32),
        compiler_params=_cparams(1),
        name="moe_experts",
    )(*plan, xs, wg, bg.reshape(N_EXPERTS, 1, f), wu, bu.reshape(N_EXPERTS, 1, f),
      wd, bd.reshape(N_EXPERTS, 1, d))


def _combine_kernel(dcur_ref, dnxt_ref, ys_ref, x1_ref, gate_ref, gt2_ref, gfin_ref, o_ref,
                    buf, sems, *, tm):
    i = pl.program_id(0)
    cur = i & 1
    slot_toks = TOP_K * tm

    def make(slot):
        def f(n, t, d):
            return pltpu.make_async_copy(_tok(ys_ref, d), _tok(buf, slot * slot_toks + t * tm + n),
                                         sems.at[slot])
        return f

    @pl.when(i == 0)
    def _():
        _start_rows(dcur_ref, tm, make(cur))

    @pl.when(i + 1 < pl.num_programs(0))
    def _():
        _start_rows(dnxt_ref, tm, make(1 - cur))

    _wait_rows(dcur_ref, tm, make(cur))
    base = pl.multiple_of(cur * slot_toks * TOK_ROWS, slot_toks * TOK_ROWS)
    gate = gate_ref[...]
    y = gate[:, 0:1] * _load_token_tiles(buf, tm, base=base)
    for t in range(1, TOP_K):
        y = y + gate[:, t:t + 1] * _load_token_tiles(buf, tm, base=base + t * tm * TOK_ROWS)
    x2 = x1_ref[...] + gt2_ref[...] * y
    o_ref[...] = _rmsnorm(x2, gfin_ref[...])


def _combine(ys, dest, x1, gate, gt2, gfin, seq, tm=TOK_TM):
    d = D_MODEL
    n = x1.shape[0]
    tm = min(tm, n)
    nt = n // tm
    row = lambda w: pl.BlockSpec((tm, w), lambda i: (i, 0))
    if seq:
        mod = pl.BlockSpec((None, 1, d), lambda i: (i // (seq // tm), 0, 0))
    else:
        mod = row(d)
    dspec = lambda fn: pl.BlockSpec((None, 1, tm * TOP_K), lambda i: (fn(i), 0, 0), memory_space=pltpu.SMEM)
    dest3 = dest.reshape(nt, 1, tm * TOP_K)
    return pl.pallas_call(
        functools.partial(_combine_kernel, tm=tm),
        grid=(nt,),
        in_specs=[dspec(lambda i: i), dspec(lambda i: jnp.minimum(i + 1, nt - 1)),
                  pl.BlockSpec(memory_space=pl.ANY), row(d), row(LANES), mod,
                  pl.BlockSpec((1, d), lambda i: (0, 0))],
        out_specs=row(d),
        out_shape=jax.ShapeDtypeStruct(x1.shape, F32),
        scratch_shapes=[pltpu.VMEM((2 * TOP_K * tm * TOK_ROWS, LANES), F32),
                        pltpu.SemaphoreType.DMA((2,))],
        compiler_params=_cparams(1),
        name="moe_combine",
    )(dest3, dest3, ys, x1, gate, gt2, gfin)


def _routing_plan(counts, n_pairs, tm=MOE_TM):
    pc = (counts + tm - 1) // tm * tm
    pend = jnp.cumsum(pc)
    pstart = pend - pc
    n_rows = -(-(n_pairs + N_EXPERTS * (tm - 1)) // tm) * tm
    n_tiles = n_rows // tm
    tile_row = jnp.arange(n_tiles, dtype=I32) * tm
    last_used = jnp.max(jnp.where(pc > 0, jnp.arange(N_EXPERTS, dtype=I32), 0))
    tile_e = jnp.minimum(jnp.sum((tile_row[:, None] >= pend[None, :]).astype(I32), axis=1), last_used)
    tile_valid = (tile_row < pend[-1]).astype(I32)
    ids = jnp.arange(N_EXPERTS, dtype=I32)
    used = pc > 0
    slot_e = (jnp.cumsum(used.astype(I32)) - 1) & 1
    later = jnp.where(used[None, :] & (ids[None, :] > ids[:, None]), ids[None, :], N_EXPERTS)
    next_e = jnp.min(later, axis=1)
    next_e = jnp.where(next_e == N_EXPERTS, -1, next_e)
    pick = (tile_e[:, None] == ids[None, :]).astype(I32)
    plan = (tile_e, tile_valid, jnp.sum(pick * next_e[None, :], axis=1),
            jnp.sum(pick * slot_e[None, :], axis=1))
    pad = ((pstart + counts).astype(I32), (pc - counts).astype(I32),
           (pend[-1:] // tm).astype(I32))
    return pstart.astype(I32), plan, n_rows, pad


def _rope_tables(pos):
    half = HEAD_DIM // 2
    inv = ROPE_THETA ** (-jnp.arange(half, dtype=F32) / half)
    ang = pos.astype(F32)[:, None] * inv[None, :]
    cos = jnp.cos(ang)
    sin = jnp.sin(ang)
    return jnp.concatenate([cos, cos, cos, cos], axis=1), jnp.concatenate([-sin, sin, -sin, sin], axis=1)


def kernel(x_prompt, x_sample, cache_k, cache_v, state_conv, c_prompt, c_sample, w_ada, b_ada,
           g_norm_mix, w_in, conv_w, g_attn_out, g_conv_out, w_out, g_norm_ffn, w_router, b_router,
           w_gate, b_gate, w_up, b_up, w_down, b_down, g_final):
    depth = w_in.shape[0]
    assert depth == 1, "single-layer trunk"
    bp, sp, d = x_prompt.shape
    bs, ts, _ = x_sample.shape
    ns = bs * ts
    n_cache = cache_k.shape[2]
    l = 0

    mods = _modulations(jnp.concatenate([c_prompt, c_sample], axis=0), w_ada[l], b_ada[l])
    mp = [m.reshape(bp, 1, d) for m in jnp.split(mods[:bp], 6, axis=-1)]
    ms = [jnp.repeat(m, ts, axis=0) for m in jnp.split(mods[bp:], 6, axis=-1)]

    w_in_bf = w_in[l].astype(BF16)
    w_out_bf = w_out[l].astype(BF16)
    wr_pad = jnp.pad(w_router[l], ((0, 0), (0, LANES - N_EXPERTS)))
    br_pad = jnp.pad(b_router[l].reshape(1, N_EXPERTS), ((0, 0), (0, LANES - N_EXPERTS)),
                     constant_values=NEG)
    g_mix = g_norm_mix[l].reshape(1, d)
    g_ffn = g_norm_ffn[l].reshape(1, d)
    g_att = g_attn_out[l].reshape(1, ATT_W)
    g_cnv = g_conv_out[l].reshape(1, CONV_W)
    g_fin = g_final.reshape(1, d)

    cos_p, sin_p = _rope_tables(jnp.arange(sp))
    q_p, k_p, v_p, conv_p, tail_p = _inproj_prompt(x_prompt, mp[0], mp[1], g_mix, w_in_bf,
                                                   cos_p, sin_p, conv_w[l])
    att_p = _attn_prompt(q_p, k_p, v_p)

    xs_rows = x_sample.reshape(ns, d)
    cos_s, sin_s = _rope_tables(PAST_LEN + jnp.arange(ts))
    cos_s = jnp.tile(cos_s, (bs, 1))
    sin_s = jnp.tile(sin_s, (bs, 1))
    st = state_conv[l]
    zrow = jnp.zeros((bs, 1, CONV_W), F32)
    s1 = jnp.concatenate([st[:, 1:2], zrow, zrow, zrow], axis=1).reshape(ns, CONV_W)
    s2 = jnp.concatenate([st[:, 0:1], st[:, 1:2], zrow, zrow], axis=1).reshape(ns, CONV_W)
    q_s, k_s, v_s, conv_s, cu_s = _inproj_sample(xs_rows, ms[0], ms[1], g_mix, w_in_bf,
                                                 cos_s, sin_s, conv_w[l], s1, s2, ts)
    to_t = lambda a: jnp.pad(jnp.transpose(a.reshape(bs, ts, N_HEADS, HEAD_DIM), (0, 2, 3, 1)),
                             ((0, 0), (0, 0), (0, 0), (0, LANES - ts)))
    q8 = jnp.pad(q_s.reshape(bs, ts, ATT_W), ((0, 0), (0, 8 - ts), (0, 0)))
    kt = jnp.transpose(cache_k[l], (0, 2, 3, 1))
    vt = jnp.transpose(cache_v[l], (0, 2, 3, 1))
    att_s = _attn_sample(q8, kt, vt, to_t(k_s), to_t(v_s), ts)[:, :ts].reshape(ns, ATT_W)

    cnt0 = jnp.zeros((8, LANES), F32)
    x1_p, h2_p, idx_p, gate_p, rank_p, cnt_p = _merge(
        att_p, conv_p, x_prompt, mp[2], mp[3], mp[4], g_att, g_cnv, g_ffn,
        w_out_bf, wr_pad, br_pad, cnt0, TOK_TM)
    x1_s, h2_s, idx_s, gate_s, rank_s, cnt_all = _merge(
        att_s, conv_s, xs_rows, ms[2], ms[3], ms[4], g_att, g_cnv, g_ffn,
        w_out_bf, wr_pad, br_pad, cnt_p, ns)

    n_tok = bp * sp + ns
    counts = cnt_all[0, :N_EXPERTS].astype(I32)
    pstart, plan, n_rows, pad = _routing_plan(counts, n_tok * TOP_K)
    ids = jnp.arange(N_EXPERTS, dtype=I32)
    slot_of = lambda idx, rank: jnp.sum(
        jnp.where(idx[..., :TOP_K, None] == ids, pstart, 0), axis=-1) + rank[..., :TOP_K]
    dest_p = slot_of(idx_p, rank_p).reshape(bp * sp, TOP_K)
    dest_s = slot_of(idx_s, rank_s)
    xs_sorted = _dispatch(h2_p.reshape(bp * sp * TOK_ROWS, LANES), dest_p, h2_s, dest_s, pad, n_rows)
    ys = _experts(xs_sorted, plan, w_gate[l], b_gate[l], w_up[l], b_up[l], w_down[l], b_down[l])
    y_prompt = _combine(ys, dest_p, x1_p.reshape(bp * sp, d), gate_p.reshape(bp * sp, LANES),
                        mp[5], g_fin, sp).reshape(bp, sp, d)
    y_sample = _combine(ys, dest_s, x1_s, gate_s, ms[5], g_fin, 0).reshape(bs, ts, d)

    heads = lambda a, b, s: a.reshape(1, b, s, N_HEADS, HEAD_DIM)
    keep = min(WINDOW_MAX, sp)
    return (y_prompt, y_sample,
            heads(k_p, bp, sp)[:, :, sp - keep:], heads(v_p, bp, sp)[:, :, sp - keep:],
            tail_p[:, 8 - (CONV_K - 1):][None],
            heads(k_s, bs, ts), heads(v_s, bs, ts),
            cu_s.reshape(bs, ts, CONV_W)[:, ts - (CONV_K - 1):][None])
```

```python
import functools

import jax
import jax.numpy as jnp
import numpy as np
from jax import lax
from jax.experimental import pallas as pl
from jax.experimental.pallas import tpu as pltpu

F32 = jnp.float32
BF16 = jnp.bfloat16
I32 = jnp.int32
HIGHEST = lax.Precision.HIGHEST

D_MODEL = 1024
HEAD_DIM = 64
N_HEADS = 12
ATT_W = N_HEADS * HEAD_DIM
CONV_W = D_MODEL - ATT_W
CONV_K = 3
PATTERNS = ((128, 1), (512, 4), (2048, 16))
WINDOW_MAX = 2048
PAST_LEN = 16384
ROPE_THETA = 10000.0
N_EXPERTS = 32
TOP_K = 4
SWIGLU_ALPHA = 1.702
SWIGLU_LIMIT = 7.0
NORM_EPS = 1e-6
NEG = -1e30
IN_W = 3 * ATT_W + 3 * CONV_W

LANES = 128
Q_BLK = 128
NEAR_W = 512
NEAR_KEYS = NEAR_W + Q_BLK
FAR_D = 16
MOE_TM = 256
TOK_TM = 256
VMEM_LIMIT = 56 * 1024 * 1024
TOK_ROWS = D_MODEL // LANES


def _cparams(n_axes, vmem=VMEM_LIMIT):
    return pltpu.CompilerParams(dimension_semantics=("arbitrary",) * n_axes,
                                vmem_limit_bytes=vmem)


def _multiplicity(delta):
    delta = np.asarray(delta)
    c = np.zeros(delta.shape, np.float32)
    for w, d in PATTERNS:
        c += ((delta >= 0) & (delta <= w) & (delta % d == 0)).astype(np.float32)
    return c


def _ada_kernel(c_ref, w_ref, b_ref, o_ref):
    c = c_ref[...]
    s = c / (1.0 + jnp.exp(-c))
    o_ref[...] = jnp.dot(s, w_ref[...], precision=HIGHEST,
                         preferred_element_type=F32) + b_ref[...]


def _modulations(c_all, w_ada, b_ada):
    r, d = c_all.shape
    n = w_ada.shape[1]
    tn = 1536
    return pl.pallas_call(
        _ada_kernel,
        grid=(n // tn,),
        in_specs=[pl.BlockSpec((r, d), lambda j: (0, 0)),
                  pl.BlockSpec((d, tn), lambda j: (0, j)),
                  pl.BlockSpec((1, tn), lambda j: (0, j))],
        out_specs=pl.BlockSpec((r, tn), lambda j: (0, j)),
        out_shape=jax.ShapeDtypeStruct((r, n), F32),
        compiler_params=_cparams(1),
        name="ada_modulation",
    )(c_all, w_ada, b_ada.reshape(1, n))


def _norm_mod(x, g, shift, scale):
    ms = jnp.mean(x * x, axis=-1, keepdims=True)
    return (x * lax.rsqrt(ms + NORM_EPS) * g) * (1.0 + scale) + shift


def _rmsnorm(x, g):
    ms = jnp.mean(x * x, axis=-1, keepdims=True)
    return x * lax.rsqrt(ms + NORM_EPS) * g


def _store_token_tiles(ref, x, base=0):
    tm = x.shape[0]
    for c in range(TOK_ROWS):
        ref[pl.ds(base + c, tm, stride=TOK_ROWS), :] = x[:, c * LANES:(c + 1) * LANES]


def _load_token_tiles(ref, tm, base=0):
    return jnp.concatenate([ref[pl.ds(base + c, tm, stride=TOK_ROWS), :] for c in range(TOK_ROWS)],
                           axis=1)


def _swap_halves(xc):
    lane = lax.broadcasted_iota(I32, xc.shape, 1)
    first = (lane & (HEAD_DIM - 1)) < HEAD_DIM // 2
    return jnp.where(first, pltpu.roll(xc, LANES - HEAD_DIM // 2, 1),
                     pltpu.roll(xc, HEAD_DIM // 2, 1))


def _rope(x, cosf, sinf):
    outs = []
    for c in range(x.shape[1] // LANES):
        xc = x[:, c * LANES:(c + 1) * LANES]
        outs.append(xc * cosf + _swap_halves(xc) * sinf)
    return jnp.concatenate(outs, axis=1)


def _inproj_kernel(*refs, tm, sample, seq_per_batch):
    if sample:
        (x_ref, sh_ref, sc_ref, g_ref, w_ref, cos_ref, sin_ref, cw_ref, s1_ref, s2_ref,
         q_ref, k_ref, v_ref, conv_ref, cu_ref, cu_ext) = refs
    else:
        (x_ref, sh_ref, sc_ref, g_ref, w_ref, cos_ref, sin_ref, cw_ref,
         q_ref, k_ref, v_ref, conv_ref, tail_ref, cu_ext) = refs
    h = _norm_mod(x_ref[...], g_ref[...], sh_ref[...], sc_ref[...])
    z = jnp.dot(h.astype(BF16), w_ref[...], preferred_element_type=F32)
    cosf = cos_ref[...]
    sinf = sin_ref[...]
    q_ref[...] = _rope(z[:, 0:ATT_W], cosf, sinf) * (HEAD_DIM ** -0.5)
    k_ref[...] = _rope(z[:, ATT_W:2 * ATT_W], cosf, sinf)
    v_ref[...] = z[:, 2 * ATT_W:3 * ATT_W]
    o = 3 * ATT_W
    gb = z[:, o:o + CONV_W]
    cu = z[:, o + CONV_W:o + 2 * CONV_W] * z[:, o + 2 * CONV_W:o + 3 * CONV_W]
    if sample:
        cu_ext[0:8, :] = jnp.zeros((8, CONV_W), F32)
    else:
        @pl.when(pl.program_id(1) == 0)
        def _():
            cu_ext[0:8, :] = jnp.zeros((8, CONV_W), F32)
    cu_ext[8:8 + tm, :] = cu
    p1 = cu_ext[7:7 + tm, :]
    p2 = cu_ext[6:6 + tm, :]
    if sample:
        t = lax.broadcasted_iota(I32, (tm, CONV_W), 0) % seq_per_batch
        p1 = jnp.where(t >= 1, p1, 0.0) + s1_ref[...]
        p2 = jnp.where(t >= 2, p2, 0.0) + s2_ref[...]
        cu_ref[...] = cu
    cw = cw_ref[...]
    conv_ref[...] = gb * (cw[0:1, :] * p2 + cw[1:2, :] * p1 + cw[2:3, :] * cu)
    if not sample:
        tail = cu_ext[tm:tm + 8, :]
        tail_ref[...] = tail
        cu_ext[0:8, :] = tail


def _inproj_prompt(x, shift, scale, g, w_bf, cosf, sinf, conv_w, tm=512):
    b, s, d = x.shape
    row = lambda bi, j: (bi, j, 0)
    per_b = lambda bi, j: (bi, 0, 0)
    const = lambda bi, j: (0, 0)
    outs = pl.pallas_call(
        functools.partial(_inproj_kernel, tm=tm, sample=False, seq_per_batch=s),
        grid=(b, s // tm),
        in_specs=[pl.BlockSpec((None, tm, d), row),
                  pl.BlockSpec((None, 1, d), per_b),
                  pl.BlockSpec((None, 1, d), per_b),
                  pl.BlockSpec((1, d), const),
                  pl.BlockSpec((d, IN_W), const),
                  pl.BlockSpec((tm, LANES), lambda bi, j: (j, 0)),
                  pl.BlockSpec((tm, LANES), lambda bi, j: (j, 0)),
                  pl.BlockSpec((CONV_K, CONV_W), const)],
        out_specs=[pl.BlockSpec((None, tm, ATT_W), row),
                   pl.BlockSpec((None, tm, ATT_W), row),
                   pl.BlockSpec((None, tm, ATT_W), row),
                   pl.BlockSpec((None, tm, CONV_W), row),
                   pl.BlockSpec((None, 8, CONV_W), per_b)],
        out_shape=[jax.ShapeDtypeStruct((b, s, ATT_W), F32),
                   jax.ShapeDtypeStruct((b, s, ATT_W), F32),
                   jax.ShapeDtypeStruct((b, s, ATT_W), F32),
                   jax.ShapeDtypeStruct((b, s, CONV_W), F32),
                   jax.ShapeDtypeStruct((b, 8, CONV_W), F32)],
        scratch_shapes=[pltpu.VMEM((tm + 8, CONV_W), F32)],
        compiler_params=_cparams(2),
        name="inproj_prompt",
    )(x, shift, scale, g, w_bf, cosf, sinf, conv_w)
    return outs


def _inproj_sample(x, shift, scale, g, w_bf, cosf, sinf, conv_w, s1, s2, seq_per_batch):
    n, d = x.shape
    full = lambda shape: pl.BlockSpec(shape, lambda i: (0, 0))
    outs = pl.pallas_call(
        functools.partial(_inproj_kernel, tm=n, sample=True, seq_per_batch=seq_per_batch),
        grid=(1,),
        in_specs=[full((n, d)), full((n, d)), full((n, d)), full((1, d)), full((d, IN_W)),
                  full((n, LANES)), full((n, LANES)), full((CONV_K, CONV_W)),
                  full((n, CONV_W)), full((n, CONV_W))],
        out_specs=[full((n, ATT_W)), full((n, ATT_W)), full((n, ATT_W)),
                   full((n, CONV_W)), full((n, CONV_W))],
        out_shape=[jax.ShapeDtypeStruct((n, ATT_W), F32)] * 3
        + [jax.ShapeDtypeStruct((n, CONV_W), F32)] * 2,
        scratch_shapes=[pltpu.VMEM((n + 8, CONV_W), F32)],
        compiler_params=_cparams(1),
        name="inproj_sample",
    )(x, shift, scale, g, w_bf, cosf, sinf, conv_w, s1, s2)
    return outs


def _stack_heads(x):
    lo = lax.broadcasted_iota(I32, x.shape, 1) < HEAD_DIM
    zero = jnp.zeros_like(x)
    return jnp.concatenate([jnp.where(lo, x, zero), jnp.where(lo, zero, x)], axis=0)


def _split_pv(r0, r1, m, rows):
    lo = lax.broadcasted_iota(I32, r0.shape, 1) < HEAD_DIM
    num = jnp.where(lo, r0, r1)
    den = jnp.where(lo, pltpu.roll(r0, HEAD_DIM, 1), pltpu.roll(r1, HEAD_DIM, 1))
    mx = jnp.where(lo, jnp.broadcast_to(m[:rows], r0.shape), jnp.broadcast_to(m[rows:], r0.shape))
    return num, den, mx


def _attn_prompt_kernel(q_ref, k_ref, v_ref, lmult_ref, o_ref,
                        kpad, v0pad, v1pad, num3, den3, max3,
                        *, seq, far_unroll, near_unroll, nq, far_s):
    nt = (((1,), (1,)), ((), ()))
    n_blk = seq // Q_BLK
    lo1 = lax.broadcasted_iota(I32, (Q_BLK, LANES), 1) < HEAD_DIM
    ones = jnp.ones((Q_BLK, LANES), F32)

    zpad = jnp.zeros((NEAR_W, LANES), BF16)
    kpad[0:NEAR_W, :] = zpad
    v0pad[0:NEAR_W, :] = zpad
    v1pad[0:NEAR_W, :] = zpad

    def fill(i, c):
        s0 = pl.multiple_of(i * Q_BLK, Q_BLK)
        d0 = pl.multiple_of(i * Q_BLK + NEAR_W, Q_BLK)
        kpad[pl.ds(d0, Q_BLK), :] = k_ref[pl.ds(s0, Q_BLK), :].astype(BF16)
        vb = v_ref[pl.ds(s0, Q_BLK), :]
        v0pad[pl.ds(d0, Q_BLK), :] = jnp.where(lo1, vb, ones).astype(BF16)
        v1pad[pl.ds(d0, Q_BLK), :] = jnp.where(lo1, ones, vb).astype(BF16)
        return c
    lax.fori_loop(0, n_blk, fill, 0)

    nf = seq // far_s
    row = lax.broadcasted_iota(I32, (2 * nf, nf), 0) & (nf - 1)
    col = lax.broadcasted_iota(I32, (2 * nf, nf), 1)
    allowed = jnp.logical_and(col <= row, ((row - col) & (FAR_D // far_s - 1)) == 0)
    lof = lax.broadcasted_iota(I32, (nf, LANES), 1) < HEAD_DIM
    onef = jnp.ones((nf, LANES), F32)

    def far(r, c):
        sl = pl.ds(r, nf, stride=far_s)
        q2 = _stack_heads(q_ref[sl, :]).astype(BF16)
        kr = k_ref[sl, :].astype(BF16)
        vr = v_ref[sl, :]
        s = lax.dot_general(q2, kr, nt, preferred_element_type=F32)
        s = jnp.where(allowed, s, NEG)
        m = jnp.max(s, axis=1, keepdims=True)
        p = jnp.exp(s - m).astype(BF16)
        r0 = jnp.dot(p[:nf], jnp.where(lof, vr, onef).astype(BF16), preferred_element_type=F32)
        r1 = jnp.dot(p[nf:], jnp.where(lof, onef, vr).astype(BF16), preferred_element_type=F32)
        num, den, mx = _split_pv(r0, r1, m, nf)
        num3[sl, :] = num
        den3[sl, :] = den
        max3[sl, :] = mx
        return c
    lax.fori_loop(0, far_s, far, 0, unroll=far_unroll)

    nk = NEAR_W + nq
    kcol = lax.broadcasted_iota(I32, (1, nk), 1)

    def near(first_blocks):
        def body(i, c):
            s0 = pl.multiple_of(i * nq, nq)
            q2 = _stack_heads(q_ref[pl.ds(s0, nq), :]).astype(BF16)
            kw = kpad[pl.ds(s0, nk), :]
            s = lax.dot_general(q2, kw, nt, preferred_element_type=F32) + lmult_ref[...]
            if first_blocks:
                s = jnp.where(kcol >= NEAR_W - s0, s, NEG)
            m = jnp.max(s, axis=1, keepdims=True)
            p = jnp.exp(s - m).astype(BF16)
            r0 = jnp.dot(p[:nq], v0pad[pl.ds(s0, nk), :], preferred_element_type=F32)
            r1 = jnp.dot(p[nq:], v1pad[pl.ds(s0, nk), :], preferred_element_type=F32)
            num, den, mx = _split_pv(r0, r1, m, nq)
            mx3 = max3[pl.ds(s0, nq), :]
            mm = jnp.maximum(mx, mx3)
            a = jnp.exp(mx - mm)
            b = jnp.exp(mx3 - mm)
            o_ref[pl.ds(s0, nq), :] = ((num * a + num3[pl.ds(s0, nq), :] * b)
                                       / (den * a + den3[pl.ds(s0, nq), :] * b))
            return c
        return body

    n_first = NEAR_W // nq
    lax.fori_loop(0, n_first, near(True), 0, unroll=near_unroll)
    lax.fori_loop(n_first, seq // nq, near(False), 0, unroll=near_unroll)


def _near_table(nq):
    i = np.arange(nq)[:, None]
    kl = np.arange(NEAR_W + nq)[None, :]
    delta = i + NEAR_W - kl
    mult = np.zeros(delta.shape, np.float32)
    for w, d in PATTERNS[:2]:
        mult += ((delta >= 0) & (delta <= w) & (delta % d == 0)).astype(np.float32)
    lmult = np.where(mult > 0, np.log(np.maximum(mult, 1.0)), NEG).astype(np.float32)
    return np.tile(lmult, (2, 1))


def _attn_prompt(q, k, v, far_unroll=4, near_unroll=2, nq=2 * Q_BLK, far_s=FAR_D):
    b, s, _ = q.shape
    lmult = _near_table(nq)
    blk = pl.BlockSpec((None, s, LANES), lambda bi, hp: (bi, 0, hp))
    tab = pl.BlockSpec((2 * nq, NEAR_W + nq), lambda bi, hp: (0, 0))
    return pl.pallas_call(
        functools.partial(_attn_prompt_kernel, seq=s, far_unroll=far_unroll, near_unroll=near_unroll,
                          nq=nq, far_s=far_s),
        grid=(b, ATT_W // LANES),
        in_specs=[blk, blk, blk, tab],
        out_specs=blk,
        out_shape=jax.ShapeDtypeStruct((b, s, ATT_W), F32),
        scratch_shapes=[pltpu.VMEM((s + NEAR_W, LANES), BF16)] * 3
        + [pltpu.VMEM((s, LANES), F32)] * 3,
        compiler_params=_cparams(2),
        name="attn_prompt",
    )(q, k, v, jnp.asarray(lmult))


def _attn_sample_kernel(q_ref, kt_ref, vt_ref, knt_ref, vnt_ref, bias_ref, mult_ref, o_ref):
    nt = (((1,), (1,)), ((), ()))
    q = q_ref[...]
    bias = bias_ref[...]
    mult = mult_ref[...]
    n_c = kt_ref.shape[-1]
    for h in range(N_HEADS):
        qh = q[:, h * HEAD_DIM:(h + 1) * HEAD_DIM].astype(BF16)
        s = jnp.concatenate(
            [jnp.dot(qh, kt_ref[h].astype(BF16), preferred_element_type=F32),
             jnp.dot(qh, knt_ref[h].astype(BF16), preferred_element_type=F32)], axis=1) + bias
        m = jnp.max(s, axis=1, keepdims=True)
        p = jnp.exp(s - m) * mult
        den = jnp.sum(p, axis=1, keepdims=True)
        pb = p.astype(BF16)
        num = (lax.dot_general(pb[:, :n_c], vt_ref[h].astype(BF16), nt, preferred_element_type=F32)
               + lax.dot_general(pb[:, n_c:], vnt_ref[h].astype(BF16), nt, preferred_element_type=F32))
        o_ref[:, h * HEAD_DIM:(h + 1) * HEAD_DIM] = num / den


def _sample_tables(n_cache, t_new):
    t = np.arange(8)[:, None] % t_new
    rho = np.arange(n_cache)[None, :]
    c_cache = _multiplicity(t + n_cache - rho)
    tp = np.arange(LANES)[None, :]
    c_new = np.where(tp < t_new, _multiplicity(t - tp), 0.0)
    mult = np.concatenate([c_cache, c_new], axis=1).astype(np.float32)
    bias = np.where(mult > 0, 0.0, NEG).astype(np.float32)
    return bias, mult


def _attn_sample(q8, kt, vt, knt, vnt, t_new):
    b = q8.shape[0]
    n_c = kt.shape[-1]
    bias, mult = _sample_tables(n_c, t_new)
    cache = pl.BlockSpec((None, N_HEADS, HEAD_DIM, n_c), lambda i: (i, 0, 0, 0))
    new = pl.BlockSpec((None, N_HEADS, HEAD_DIM, LANES), lambda i: (i, 0, 0, 0))
    row = pl.BlockSpec((None, 8, ATT_W), lambda i: (i, 0, 0))
    tab = pl.BlockSpec((8, n_c + LANES), lambda i: (0, 0))
    return pl.pallas_call(
        _attn_sample_kernel,
        grid=(b,),
        in_specs=[row, cache, cache, new, new, tab, tab],
        out_specs=row,
        out_shape=jax.ShapeDtypeStruct((b, 8, ATT_W), F32),
        compiler_params=_cparams(1),
        name="attn_sample",
    )(q8, kt, vt, knt, vnt, jnp.asarray(bias), jnp.asarray(mult))


def _merge_kernel(att_ref, conv_ref, x_ref, gt1_ref, sh2_ref, sc2_ref, ga_ref, gc_ref, gf_ref,
                  wo_ref, wr_ref, wrl_ref, br_ref, cnt0_ref, tri_ref,
                  x1_ref, h2_ref, idx_ref, gate_ref, rank_ref, cnt_ref, cnt_sc, *, n_axes):
    first = pl.program_id(0) == 0
    if n_axes == 2:
        first = jnp.logical_and(first, pl.program_id(1) == 0)

    @pl.when(first)
    def _():
        cnt_sc[...] = cnt0_ref[...]

    an = _rmsnorm(att_ref[...], ga_ref[...]).astype(BF16)
    cn = _rmsnorm(conv_ref[...], gc_ref[...]).astype(BF16)
    mix = (jnp.dot(an, wo_ref[0:ATT_W, :], preferred_element_type=F32)
           + jnp.dot(cn, wo_ref[ATT_W:D_MODEL, :], preferred_element_type=F32))
    x1 = x_ref[...] + gt1_ref[...] * mix
    x1_ref[...] = x1
    h2 = _norm_mod(x1, gf_ref[...], sh2_ref[...], sc2_ref[...])
    _store_token_tiles(h2_ref, h2)
    hi = h2.astype(BF16)
    lo = (h2 - hi.astype(F32)).astype(BF16)
    logits = (jnp.dot(hi, wr_ref[...], preferred_element_type=F32)
              + (jnp.dot(hi, wrl_ref[...], preferred_element_type=F32)
                 + jnp.dot(lo, wr_ref[...], preferred_element_type=F32))) + br_ref[...]
    tm = logits.shape[0]
    lane = lax.broadcasted_iota(I32, (tm, LANES), 1)
    work = logits
    vals, idxs = [], []
    for _ in range(TOP_K):
        mx = jnp.max(work, axis=1, keepdims=True)
        ix = jnp.min(jnp.where(work == mx, lane, LANES), axis=1, keepdims=True)
        vals.append(mx)
        idxs.append(ix)
        work = jnp.where(lane == ix, 3.0 * NEG, work)
    es = [jnp.exp(v - vals[0]) for v in vals]
    den = es[0] + es[1] + es[2] + es[3]
    onehot = jnp.zeros((tm, LANES), F32)
    for ix in idxs:
        onehot = onehot + (lane == ix).astype(F32)
    before = jnp.dot(tri_ref[...], onehot.astype(BF16), preferred_element_type=F32) + cnt_sc[0:1, :]
    idx_o = jnp.zeros((tm, LANES), I32)
    gate_o = jnp.zeros((tm, LANES), F32)
    rank_o = jnp.zeros((tm, LANES), F32)
    for t in range(TOP_K):
        rk = jnp.sum(jnp.where(lane == idxs[t], before, 0.0), axis=1, keepdims=True)
        idx_o = jnp.where(lane == t, idxs[t], idx_o)
        gate_o = jnp.where(lane == t, es[t] / den, gate_o)
        rank_o = jnp.where(lane == t, rk, rank_o)
    idx_ref[...] = idx_o
    gate_ref[...] = gate_o
    rank_ref[...] = rank_o.astype(I32)
    cnt_sc[...] = cnt_sc[...] + jnp.sum(onehot, axis=0, keepdims=True)
    cnt_ref[...] = cnt_sc[...]


def _merge(att, conv, x, gt1, sh2, sc2, ga, gc, gf, wo_bf, wr_pad, br_pad, cnt0, tm):
    d = D_MODEL
    wr_hi = wr_pad.astype(BF16)
    wr_lo = (wr_pad - wr_hi.astype(F32)).astype(BF16)
    tri = jnp.asarray(np.tril(np.ones((tm, tm), np.float32), -1), BF16)
    if att.ndim == 3:
        b, s, _ = att.shape
        grid = (b, s // tm)
        row = lambda w: pl.BlockSpec((None, tm, w), lambda bi, j: (bi, j, 0))
        mod = pl.BlockSpec((None, 1, d), lambda bi, j: (bi, 0, 0))
        const = lambda shape: pl.BlockSpec(shape, lambda bi, j: (0, 0))
        lead = (b, s)
        tiles = pl.BlockSpec((None, tm * TOK_ROWS, LANES), lambda bi, j: (bi, j, 0))
        tiles_shape = (b, s * TOK_ROWS, LANES)
    else:
        n = att.shape[0]
        grid = (n // tm,)
        row = lambda w: pl.BlockSpec((tm, w), lambda i: (i, 0))
        mod = row(d)
        const = lambda shape: pl.BlockSpec(shape, lambda i: (0, 0))
        lead = (n,)
        tiles = pl.BlockSpec((tm * TOK_ROWS, LANES), lambda i: (i, 0))
        tiles_shape = (n * TOK_ROWS, LANES)
    return pl.pallas_call(
        functools.partial(_merge_kernel, n_axes=len(grid)),
        grid=grid,
        in_specs=[row(ATT_W), row(CONV_W), row(d), mod, mod, mod,
                  const((1, ATT_W)), const((1, CONV_W)), const((1, d)),
                  const((d, d)), const((d, LANES)), const((d, LANES)), const((1, LANES)),
                  const((8, LANES)), const((tm, tm))],
        out_specs=[row(d), tiles, row(LANES), row(LANES), row(LANES), const((8, LANES))],
        out_shape=[jax.ShapeDtypeStruct(lead + (d,), F32),
                   jax.ShapeDtypeStruct(tiles_shape, F32),
                   jax.ShapeDtypeStruct(lead + (LANES,), I32),
                   jax.ShapeDtypeStruct(lead + (LANES,), F32),
                   jax.ShapeDtypeStruct(lead + (LANES,), I32),
                   jax.ShapeDtypeStruct((8, LANES), F32)],
        scratch_shapes=[pltpu.VMEM((8, LANES), F32)],
        compiler_params=_cparams(len(grid)),
        name="merge_route",
    )(att, conv, x, gt1, sh2, sc2, ga, gc, gf, wo_bf, wr_hi, wr_lo, br_pad, cnt0, tri)


def _tok(ref, n, count=1):
    return ref.at[pl.ds(pl.multiple_of(n * TOK_ROWS, TOK_ROWS), count * TOK_ROWS)]


ROW_DMA_UNROLL = 4


def _start_rows(dest_ref, tm, make):
    def start(n, c):
        for t in range(TOP_K):
            make(n, t, dest_ref[0, n * TOP_K + t]).start(priority=t % 2)
        return c
    lax.fori_loop(0, tm, start, 0, unroll=ROW_DMA_UNROLL)


def _wait_rows(dest_ref, tm, make):
    def wait(n, c):
        for t in range(TOP_K):
            make(n, t, dest_ref[0, n * TOP_K + t]).wait()
        return c
    lax.fori_loop(0, tm, wait, 0, unroll=2 * ROW_DMA_UNROLL)


def _row_copies(dest_ref, tm, make):
    _start_rows(dest_ref, tm, make)
    _wait_rows(dest_ref, tm, make)


def _dispatch_kernel(ps_ref, pn_ref, t0_ref, dest_ref, h_ref, dest_s_ref, hs_ref, xs_ref,
                     zbuf, sem, *, tm, n_s, row_tm, n_tiles):
    def make(n, t, d):
        return pltpu.make_async_copy(_tok(h_ref, n), _tok(xs_ref, d), sem)
    _row_copies(dest_ref, tm, make)

    @pl.when(pl.program_id(0) == pl.num_programs(0) - 1)
    def _():
        def make_s(n, t, d):
            return pltpu.make_async_copy(_tok(hs_ref, n), _tok(xs_ref, d), sem)
        _row_copies(dest_s_ref, n_s, make_s)
        _zero_fill(ps_ref, pn_ref, t0_ref, xs_ref, zbuf, sem, row_tm, n_tiles)


def _dispatch(h2, dest, h2_s, dest_s, pad, n_rows, tm=2 * TOK_TM, row_tm=MOE_TM):
    n = h2.shape[0] // TOK_ROWS
    n_s = h2_s.shape[0] // TOK_ROWS
    nt = n // tm
    return pl.pallas_call(
        functools.partial(_dispatch_kernel, tm=tm, n_s=n_s, row_tm=row_tm, n_tiles=n_rows // row_tm),
        grid_spec=pltpu.PrefetchScalarGridSpec(
            num_scalar_prefetch=3,
            grid=(nt,),
            in_specs=[pl.BlockSpec((None, 1, tm * TOP_K), lambda i, *_: (i, 0, 0), memory_space=pltpu.SMEM),
                      pl.BlockSpec((tm * TOK_ROWS, LANES), lambda i, *_: (i, 0)),
                      pl.BlockSpec((1, n_s * TOP_K), lambda i, *_: (0, 0), memory_space=pltpu.SMEM),
                      pl.BlockSpec((n_s * TOK_ROWS, LANES), lambda i, *_: (0, 0))],
            out_specs=pl.BlockSpec(memory_space=pl.ANY),
            scratch_shapes=[pltpu.VMEM((row_tm * TOK_ROWS, LANES), F32), pltpu.SemaphoreType.DMA(())],
        ),
        out_shape=jax.ShapeDtypeStruct((n_rows * TOK_ROWS, LANES), F32),
        compiler_params=_cparams(1),
        name="moe_dispatch",
    )(*pad, dest.reshape(nt, 1, tm * TOP_K), h2, dest_s.reshape(1, n_s * TOP_K), h2_s)


def _zero_fill(ps_ref, pn_ref, t0_ref, xs_ref, zbuf, sem, tm, n_tiles):
    zbuf[...] = jnp.zeros_like(zbuf)
    bits = [tm >> (k + 1) for k in range(tm.bit_length() - 1)]

    def segments(act):
        def seg(e, c):
            off = ps_ref[e]
            ln = pn_ref[e]
            for bit in bits:
                @pl.when((ln & bit) != 0)
                def _(off=off, bit=bit):
                    act(pltpu.make_async_copy(_tok(zbuf, 0, bit), _tok(xs_ref, off, bit), sem))
                off = off + (ln & bit)
            return c
        lax.fori_loop(0, N_EXPERTS, seg, 0)

        def tile(i, c):
            act(pltpu.make_async_copy(zbuf, _tok(xs_ref, i * tm, tm), sem))
            return c
        lax.fori_loop(t0_ref[0], n_tiles, tile, 0)

    segments(lambda cp: cp.start())
    segments(lambda cp: cp.wait())


def _expert_kernel(te_ref, tv_ref, nx_ref, sl_ref, xs_ref, wg_hbm, bg_ref, wu_hbm, bu_ref,
                   wd_hbm, bd_ref, ys_ref, wbuf, wg_bf, wu_bf, wd_bf, sems, *, tm):
    i = pl.program_id(0)
    e = te_ref[i]
    slot = sl_ref[i]
    new_expert = jnp.logical_or(i == 0, e != te_ref[jnp.maximum(i - 1, 0)])

    def fetch(expert, s):
        return [pltpu.make_async_copy(w.at[expert], wbuf.at[s, k], sems.at[s, k])
                for k, w in enumerate((wg_hbm, wu_hbm, wd_hbm))]

    @pl.when(i == 0)
    def _():
        for cp in fetch(e, slot):
            cp.start()

    @pl.when(new_expert)
    def _():
        for cp in fetch(e, slot):
            cp.wait()

        @pl.when(nx_ref[i] >= 0)
        def _():
            for cp in fetch(nx_ref[i], 1 - slot):
                cp.start()
        wg_bf[...] = wbuf[slot, 0].astype(BF16)
        wu_bf[...] = wbuf[slot, 1].astype(BF16)
        wd_bf[...] = wbuf[slot, 2].astype(BF16)

    @pl.when(tv_ref[i] > 0)
    def _():
        x = _load_token_tiles(xs_ref, tm).astype(BF16)
        g = jnp.dot(x, wg_bf[...], preferred_element_type=F32) + bg_ref[...]
        u = jnp.dot(x, wu_bf[...], preferred_element_type=F32) + bu_ref[...]
        g = jnp.minimum(g, SWIGLU_LIMIT)
        u = jnp.clip(u, -SWIGLU_LIMIT, SWIGLU_LIMIT)
        act = (u + 1.0) * g * (1.0 / (1.0 + jnp.exp(-SWIGLU_ALPHA * g)))
        y = jnp.dot(act.astype(BF16), wd_bf[...], preferred_element_type=F32) + bd_ref[...]
        _store_token_tiles(ys_ref, y)

    @pl.when(tv_ref[i] == 0)
    def _():
        ys_ref[...] = jnp.zeros_like(ys_ref)


def _experts(xs, plan, wg, bg, wu, bu, wd, bd, tm=MOE_TM):
    n_tiles = xs.shape[0] // (tm * TOK_ROWS)
    d, f = wg.shape[-2:]
    assert d == f == D_MODEL
    b_spec = lambda n: pl.BlockSpec((None, 1, n), lambda i, te, *_: (te[i], 0, 0))
    rows = pl.BlockSpec((tm * TOK_ROWS, LANES), lambda i, *_: (i, 0))
    hbm = pl.BlockSpec(memory_space=pl.ANY)
    return pl.pallas_call(
        functools.partial(_expert_kernel, tm=tm),
        grid_spec=pltpu.PrefetchScalarGridSpec(
            num_scalar_prefetch=4,
            grid=(n_tiles,),
            in_specs=[rows, hbm, b_spec(f), hbm, b_spec(f), hbm, b_spec(d)],
            out_specs=rows,
            scratch_shapes=[pltpu.VMEM((2, 3, d, f), F32),
                            pltpu.VMEM((d, f), BF16), pltpu.VMEM((d, f), BF16), pltpu.VMEM((f, d), BF16),
                            pltpu.SemaphoreType.DMA((2, 3))],
        ),
        out_shape=jax.ShapeDtypeStruct(xs.shape, F32),
        compiler_params=_cparams(1),
        name="moe_experts",
    )(*plan, xs, wg, bg.reshape(N_EXPERTS, 1, f), wu, bu.reshape(N_EXPERTS, 1, f),
      wd, bd.reshape(N_EXPERTS, 1, d))


def _combine_kernel(dcur_ref, dnxt_ref, ys_ref, x1_ref, gate_ref, gt2_ref, gfin_ref, o_ref,
                    buf, sems, *, tm):
    i = pl.program_id(0)
    cur = i & 1
    slot_toks = TOP_K * tm

    def make(slot):
        def f(n, t, d):
            return pltpu.make_async_copy(_tok(ys_ref, d), _tok(buf, slot * slot_toks + t * tm + n),
                                         sems.at[slot])
        return f

    @pl.when(i == 0)
    def _():
        _start_rows(dcur_ref, tm, make(cur))

    @pl.when(i + 1 < pl.num_programs(0))
    def _():
        _start_rows(dnxt_ref, tm, make(1 - cur))

    _wait_rows(dcur_ref, tm, make(cur))
    base = pl.multiple_of(cur * slot_toks * TOK_ROWS, slot_toks * TOK_ROWS)
    gate = gate_ref[...]
    y = gate[:, 0:1] * _load_token_tiles(buf, tm, base=base)
    for t in range(1, TOP_K):
        y = y + gate[:, t:t + 1] * _load_token_tiles(buf, tm, base=base + t * tm * TOK_ROWS)
    x2 = x1_ref[...] + gt2_ref[...] * y
    o_ref[...] = _rmsnorm(x2, gfin_ref[...])


def _combine(ys, dest, x1, gate, gt2, gfin, seq, tm=TOK_TM):
    d = D_MODEL
    n = x1.shape[0]
    tm = min(tm, n)
    nt = n // tm
    row = lambda w: pl.BlockSpec((tm, w), lambda i: (i, 0))
    if seq:
        mod = pl.BlockSpec((None, 1, d), lambda i: (i // (seq // tm), 0, 0))
    else:
        mod = row(d)
    dspec = lambda fn: pl.BlockSpec((None, 1, tm * TOP_K), lambda i: (fn(i), 0, 0), memory_space=pltpu.SMEM)
    dest3 = dest.reshape(nt, 1, tm * TOP_K)
    return pl.pallas_call(
        functools.partial(_combine_kernel, tm=tm),
        grid=(nt,),
        in_specs=[dspec(lambda i: i), dspec(lambda i: jnp.minimum(i + 1, nt - 1)),
                  pl.BlockSpec(memory_space=pl.ANY), row(d), row(LANES), mod,
                  pl.BlockSpec((1, d), lambda i: (0, 0))],
        out_specs=row(d),
        out_shape=jax.ShapeDtypeStruct(x1.shape, F32),
        scratch_shapes=[pltpu.VMEM((2 * TOP_K * tm * TOK_ROWS, LANES), F32),
                        pltpu.SemaphoreType.DMA((2,))],
        compiler_params=_cparams(1),
        name="moe_combine",
    )(dest3, dest3, ys, x1, gate, gt2, gfin)


def _routing_plan(counts, n_pairs, tm=MOE_TM):
    pc = (counts + tm - 1) // tm * tm
    pend = jnp.cumsum(pc)
    pstart = pend - pc
    n_rows = -(-(n_pairs + N_EXPERTS * (tm - 1)) // tm) * tm
    n_tiles = n_rows // tm
    tile_row = jnp.arange(n_tiles, dtype=I32) * tm
    last_used = jnp.max(jnp.where(pc > 0, jnp.arange(N_EXPERTS, dtype=I32), 0))
    tile_e = jnp.minimum(jnp.sum((tile_row[:, None] >= pend[None, :]).astype(I32), axis=1), last_used)
    tile_valid = (tile_row < pend[-1]).astype(I32)
    ids = jnp.arange(N_EXPERTS, dtype=I32)
    used = pc > 0
    slot_e = (jnp.cumsum(used.astype(I32)) - 1) & 1
    later = jnp.where(used[None, :] & (ids[None, :] > ids[:, None]), ids[None, :], N_EXPERTS)
    next_e = jnp.min(later, axis=1)
    next_e = jnp.where(next_e == N_EXPERTS, -1, next_e)
    pick = (tile_e[:, None] == ids[None, :]).astype(I32)
    plan = (tile_e, tile_valid, jnp.sum(pick * next_e[None, :], axis=1),
            jnp.sum(pick * slot_e[None, :], axis=1))
    pad = ((pstart + counts).astype(I32), (pc - counts).astype(I32),
           (pend[-1:] // tm).astype(I32))
    return pstart.astype(I32), plan, n_rows, pad


def _rope_tables(pos):
    half = HEAD_DIM // 2
    inv = ROPE_THETA ** (-np.arange(half, dtype=np.float64) / half)
    ang = np.asarray(pos, np.float64)[:, None] * inv[None, :]
    cos = np.cos(ang)
    sin = np.sin(ang)
    cosf = np.concatenate([cos, cos, cos, cos], axis=1).astype(np.float32)
    sinf = np.concatenate([-sin, sin, -sin, sin], axis=1).astype(np.float32)
    return jnp.asarray(cosf), jnp.asarray(sinf)


def kernel(x_prompt, x_sample, cache_k, cache_v, state_conv, c_prompt, c_sample, w_ada, b_ada,
           g_norm_mix, w_in, conv_w, g_attn_out, g_conv_out, w_out, g_norm_ffn, w_router, b_router,
           w_gate, b_gate, w_up, b_up, w_down, b_down, g_final):
    depth = w_in.shape[0]
    assert depth == 1, "single-layer trunk"
    bp, sp, d = x_prompt.shape
    bs, ts, _ = x_sample.shape
    ns = bs * ts
    n_cache = cache_k.shape[2]
    l = 0

    mods = _modulations(jnp.concatenate([c_prompt, c_sample], axis=0), w_ada[l], b_ada[l])
    mp = [m.reshape(bp, 1, d) for m in jnp.split(mods[:bp], 6, axis=-1)]
    ms = [jnp.repeat(m, ts, axis=0) for m in jnp.split(mods[bp:], 6, axis=-1)]

    w_in_bf = w_in[l].astype(BF16)
    w_out_bf = w_out[l].astype(BF16)
    wr_pad = jnp.pad(w_router[l], ((0, 0), (0, LANES - N_EXPERTS)))
    br_pad = jnp.pad(b_router[l].reshape(1, N_EXPERTS), ((0, 0), (0, LANES - N_EXPERTS)),
                     constant_values=NEG)
    g_mix = g_norm_mix[l].reshape(1, d)
    g_ffn = g_norm_ffn[l].reshape(1, d)
    g_att = g_attn_out[l].reshape(1, ATT_W)
    g_cnv = g_conv_out[l].reshape(1, CONV_W)
    g_fin = g_final.reshape(1, d)

    cos_p, sin_p = _rope_tables(np.arange(sp))
    q_p, k_p, v_p, conv_p, tail_p = _inproj_prompt(x_prompt, mp[0], mp[1], g_mix, w_in_bf,
                                                   cos_p, sin_p, conv_w[l])
    att_p = _attn_prompt(q_p, k_p, v_p)

    xs_rows = x_sample.reshape(ns, d)
    cos_s, sin_s = _rope_tables(np.tile(PAST_LEN + np.arange(ts), bs))
    st = state_conv[l]
    zrow = jnp.zeros((bs, 1, CONV_W), F32)
    s1 = jnp.concatenate([st[:, 1:2], zrow, zrow, zrow], axis=1).reshape(ns, CONV_W)
    s2 = jnp.concatenate([st[:, 0:1], st[:, 1:2], zrow, zrow], axis=1).reshape(ns, CONV_W)
    q_s, k_s, v_s, conv_s, cu_s = _inproj_sample(xs_rows, ms[0], ms[1], g_mix, w_in_bf,
                                                 cos_s, sin_s, conv_w[l], s1, s2, ts)
    to_t = lambda a: jnp.pad(jnp.transpose(a.reshape(bs, ts, N_HEADS, HEAD_DIM), (0, 2, 3, 1)),
                             ((0, 0), (0, 0), (0, 0), (0, LANES - ts)))
    q8 = jnp.pad(q_s.reshape(bs, ts, ATT_W), ((0, 0), (0, 8 - ts), (0, 0)))
    kt = jnp.transpose(cache_k[l], (0, 2, 3, 1))
    vt = jnp.transpose(cache_v[l], (0, 2, 3, 1))
    att_s = _attn_sample(q8, kt, vt, to_t(k_s), to_t(v_s), ts)[:, :ts].reshape(ns, ATT_W)

    cnt0 = jnp.zeros((8, LANES), F32)
    x1_p, h2_p, idx_p, gate_p, rank_p, cnt_p = _merge(
        att_p, conv_p, x_prompt, mp[2], mp[3], mp[4], g_att, g_cnv, g_ffn,
        w_out_bf, wr_pad, br_pad, cnt0, 2 * TOK_TM)
    x1_s, h2_s, idx_s, gate_s, rank_s, cnt_all = _merge(
        att_s, conv_s, xs_rows, ms[2], ms[3], ms[4], g_att, g_cnv, g_ffn,
        w_out_bf, wr_pad, br_pad, cnt_p, ns)

    n_tok = bp * sp + ns
    counts = cnt_all[0, :N_EXPERTS].astype(I32)
    pstart, plan, n_rows, pad = _routing_plan(counts, n_tok * TOP_K)
    ids = jnp.arange(N_EXPERTS, dtype=I32)
    slot_of = lambda idx, rank: jnp.sum(
        jnp.where(idx[..., :TOP_K, None] == ids, pstart, 0), axis=-1) + rank[..., :TOP_K]
    dest_p = slot_of(idx_p, rank_p).reshape(bp * sp, TOP_K)
    dest_s = slot_of(idx_s, rank_s)
    xs_sorted = _dispatch(h2_p.reshape(bp * sp * TOK_ROWS, LANES), dest_p, h2_s, dest_s, pad, n_rows)
    ys = _experts(xs_sorted, plan, w_gate[l], b_gate[l], w_up[l], b_up[l], w_down[l], b_down[l])
    y_prompt = _combine(ys, dest_p, x1_p.reshape(bp * sp, d), gate_p.reshape(bp * sp, LANES),
                        mp[5], g_fin, sp).reshape(bp, sp, d)
    y_sample = _combine(ys, dest_s, x1_s, gate_s, ms[5], g_fin, 0).reshape(bs, ts, d)

    heads = lambda a, b, s: a.reshape(1, b, s, N_HEADS, HEAD_DIM)
    keep = min(WINDOW_MAX, sp)
    return (y_prompt, y_sample,
            heads(k_p, bp, sp)[:, :, sp - keep:], heads(v_p, bp, sp)[:, :, sp - keep:],
            tail_p[:, 8 - (CONV_K - 1):][None],
            heads(k_s, bs, ts), heads(v_s, bs, ts),
            cu_s.reshape(bs, ts, CONV_W)[:, ts - (CONV_K - 1):][None])
```

```python
import functools

import jax
import jax.numpy as jnp
import numpy as np
from jax import lax
from jax.experimental import pallas as pl
from jax.experimental.pallas import tpu as pltpu

F32 = jnp.float32
BF16 = jnp.bfloat16
I32 = jnp.int32
HIGHEST = lax.Precision.HIGHEST

D_MODEL = 1024
HEAD_DIM = 64
N_HEADS = 12
ATT_W = N_HEADS * HEAD_DIM
CONV_W = D_MODEL - ATT_W
CONV_K = 3
PATTERNS = ((128, 1), (512, 4), (2048, 16))
WINDOW_MAX = 2048
PAST_LEN = 16384
ROPE_THETA = 10000.0
N_EXPERTS = 32
TOP_K = 4
SWIGLU_ALPHA = 1.702
SWIGLU_LIMIT = 7.0
NORM_EPS = 1e-6
NEG = -1e30
IN_W = 3 * ATT_W + 3 * CONV_W

LANES = 128
Q_BLK = 128
NEAR_W = 512
NEAR_KEYS = NEAR_W + Q_BLK
FAR_D = 16
MOE_TM = 256
TOK_TM = 256
VMEM_LIMIT = 56 * 1024 * 1024
TOK_ROWS = D_MODEL // LANES


def _cparams(n_axes, vmem=VMEM_LIMIT):
    return pltpu.CompilerParams(dimension_semantics=("arbitrary",) * n_axes,
                                vmem_limit_bytes=vmem)


def _multiplicity(delta):
    delta = np.asarray(delta)
    c = np.zeros(delta.shape, np.float32)
    for w, d in PATTERNS:
        c += ((delta >= 0) & (delta <= w) & (delta % d == 0)).astype(np.float32)
    return c


def _ada_kernel(c_ref, w_ref, b_ref, o_ref):
    c = c_ref[...]
    s = c / (1.0 + jnp.exp(-c))
    o_ref[...] = jnp.dot(s, w_ref[...], precision=HIGHEST,
                         preferred_element_type=F32) + b_ref[...]


def _modulations(c_all, w_ada, b_ada):
    r, d = c_all.shape
    n = w_ada.shape[1]
    tn = 1536
    return pl.pallas_call(
        _ada_kernel,
        grid=(n // tn,),
        in_specs=[pl.BlockSpec((r, d), lambda j: (0, 0)),
                  pl.BlockSpec((d, tn), lambda j: (0, j)),
                  pl.BlockSpec((1, tn), lambda j: (0, j))],
        out_specs=pl.BlockSpec((r, tn), lambda j: (0, j)),
        out_shape=jax.ShapeDtypeStruct((r, n), F32),
        compiler_params=_cparams(1),
        name="ada_modulation",
    )(c_all, w_ada, b_ada.reshape(1, n))


def _norm_mod(x, g, shift, scale):
    ms = jnp.mean(x * x, axis=-1, keepdims=True)
    return (x * lax.rsqrt(ms + NORM_EPS) * g) * (1.0 + scale) + shift


def _rmsnorm(x, g):
    ms = jnp.mean(x * x, axis=-1, keepdims=True)
    return x * lax.rsqrt(ms + NORM_EPS) * g


def _store_token_tiles(ref, x, base=0):
    tm = x.shape[0]
    for c in range(TOK_ROWS):
        ref[pl.ds(base + c, tm, stride=TOK_ROWS), :] = x[:, c * LANES:(c + 1) * LANES]


def _load_token_tiles(ref, tm, base=0):
    return jnp.concatenate([ref[pl.ds(base + c, tm, stride=TOK_ROWS), :] for c in range(TOK_ROWS)],
                           axis=1)


def _swap_halves(xc):
    lane = lax.broadcasted_iota(I32, xc.shape, 1)
    first = (lane & (HEAD_DIM - 1)) < HEAD_DIM // 2
    return jnp.where(first, pltpu.roll(xc, LANES - HEAD_DIM // 2, 1),
                     pltpu.roll(xc, HEAD_DIM // 2, 1))


def _rope(x, cosf, sinf):
    outs = []
    for c in range(x.shape[1] // LANES):
        xc = x[:, c * LANES:(c + 1) * LANES]
        outs.append(xc * cosf + _swap_halves(xc) * sinf)
    return jnp.concatenate(outs, axis=1)


def _inproj_kernel(*refs, tm, sample, seq_per_batch):
    if sample:
        (x_ref, sh_ref, sc_ref, g_ref, w_ref, cos_ref, sin_ref, cw_ref, s1_ref, s2_ref,
         q_ref, k_ref, v_ref, conv_ref, cu_ref, cu_ext) = refs
    else:
        (x_ref, sh_ref, sc_ref, g_ref, w_ref, cos_ref, sin_ref, cw_ref,
         q_ref, k_ref, v_ref, conv_ref, tail_ref, cu_ext) = refs
    h = _norm_mod(x_ref[...], g_ref[...], sh_ref[...], sc_ref[...])
    z = jnp.dot(h.astype(BF16), w_ref[...], preferred_element_type=F32)
    cosf = cos_ref[...]
    sinf = sin_ref[...]
    q_ref[...] = _rope(z[:, 0:ATT_W], cosf, sinf) * (HEAD_DIM ** -0.5)
    k_ref[...] = _rope(z[:, ATT_W:2 * ATT_W], cosf, sinf)
    v_ref[...] = z[:, 2 * ATT_W:3 * ATT_W]
    o = 3 * ATT_W
    gb = z[:, o:o + CONV_W]
    cu = z[:, o + CONV_W:o + 2 * CONV_W] * z[:, o + 2 * CONV_W:o + 3 * CONV_W]
    if sample:
        cu_ext[0:8, :] = jnp.zeros((8, CONV_W), F32)
    else:
        @pl.when(pl.program_id(1) == 0)
        def _():
            cu_ext[0:8, :] = jnp.zeros((8, CONV_W), F32)
    cu_ext[8:8 + tm, :] = cu
    p1 = cu_ext[7:7 + tm, :]
    p2 = cu_ext[6:6 + tm, :]
    if sample:
        t = lax.broadcasted_iota(I32, (tm, CONV_W), 0) % seq_per_batch
        p1 = jnp.where(t >= 1, p1, 0.0) + s1_ref[...]
        p2 = jnp.where(t >= 2, p2, 0.0) + s2_ref[...]
        cu_ref[...] = cu
    cw = cw_ref[...]
    conv_ref[...] = gb * (cw[0:1, :] * p2 + cw[1:2, :] * p1 + cw[2:3, :] * cu)
    if not sample:
        tail = cu_ext[tm:tm + 8, :]
        tail_ref[...] = tail
        cu_ext[0:8, :] = tail


def _inproj_prompt(x, shift, scale, g, w_bf, cosf, sinf, conv_w, tm=512):
    b, s, d = x.shape
    row = lambda bi, j: (bi, j, 0)
    per_b = lambda bi, j: (bi, 0, 0)
    const = lambda bi, j: (0, 0)
    outs = pl.pallas_call(
        functools.partial(_inproj_kernel, tm=tm, sample=False, seq_per_batch=s),
        grid=(b, s // tm),
        in_specs=[pl.BlockSpec((None, tm, d), row),
                  pl.BlockSpec((None, 1, d), per_b),
                  pl.BlockSpec((None, 1, d), per_b),
                  pl.BlockSpec((1, d), const),
                  pl.BlockSpec((d, IN_W), const),
                  pl.BlockSpec((tm, LANES), lambda bi, j: (j, 0)),
                  pl.BlockSpec((tm, LANES), lambda bi, j: (j, 0)),
                  pl.BlockSpec((CONV_K, CONV_W), const)],
        out_specs=[pl.BlockSpec((None, tm, ATT_W), row),
                   pl.BlockSpec((None, tm, ATT_W), row),
                   pl.BlockSpec((None, tm, ATT_W), row),
                   pl.BlockSpec((None, tm, CONV_W), row),
                   pl.BlockSpec((None, 8, CONV_W), per_b)],
        out_shape=[jax.ShapeDtypeStruct((b, s, ATT_W), F32),
                   jax.ShapeDtypeStruct((b, s, ATT_W), F32),
                   jax.ShapeDtypeStruct((b, s, ATT_W), F32),
                   jax.ShapeDtypeStruct((b, s, CONV_W), F32),
                   jax.ShapeDtypeStruct((b, 8, CONV_W), F32)],
        scratch_shapes=[pltpu.VMEM((tm + 8, CONV_W), F32)],
        compiler_params=_cparams(2),
        name="inproj_prompt",
    )(x, shift, scale, g, w_bf, cosf, sinf, conv_w)
    return outs


def _inproj_sample(x, shift, scale, g, w_bf, cosf, sinf, conv_w, s1, s2, seq_per_batch):
    n, d = x.shape
    full = lambda shape: pl.BlockSpec(shape, lambda i: (0, 0))
    outs = pl.pallas_call(
        functools.partial(_inproj_kernel, tm=n, sample=True, seq_per_batch=seq_per_batch),
        grid=(1,),
        in_specs=[full((n, d)), full((n, d)), full((n, d)), full((1, d)), full((d, IN_W)),
                  full((n, LANES)), full((n, LANES)), full((CONV_K, CONV_W)),
                  full((n, CONV_W)), full((n, CONV_W))],
        out_specs=[full((n, ATT_W)), full((n, ATT_W)), full((n, ATT_W)),
                   full((n, CONV_W)), full((n, CONV_W))],
        out_shape=[jax.ShapeDtypeStruct((n, ATT_W), F32)] * 3
        + [jax.ShapeDtypeStruct((n, CONV_W), F32)] * 2,
        scratch_shapes=[pltpu.VMEM((n + 8, CONV_W), F32)],
        compiler_params=_cparams(1),
        name="inproj_sample",
    )(x, shift, scale, g, w_bf, cosf, sinf, conv_w, s1, s2)
    return outs


def _stack_heads(x):
    lo = lax.broadcasted_iota(I32, x.shape, 1) < HEAD_DIM
    zero = jnp.zeros_like(x)
    return jnp.concatenate([jnp.where(lo, x, zero), jnp.where(lo, zero, x)], axis=0)


def _split_pv(r0, r1, m, rows):
    lo = lax.broadcasted_iota(I32, r0.shape, 1) < HEAD_DIM
    num = jnp.where(lo, r0, r1)
    den = jnp.where(lo, pltpu.roll(r0, HEAD_DIM, 1), pltpu.roll(r1, HEAD_DIM, 1))
    mx = jnp.where(lo, jnp.broadcast_to(m[:rows], r0.shape), jnp.broadcast_to(m[rows:], r0.shape))
    return num, den, mx


def _attn_prompt_kernel(q_ref, k_ref, v_ref, lmult_ref, o_ref,
                        kpad, v0pad, v1pad, num3, den3, max3,
                        *, seq, far_unroll, near_unroll, nq, far_s):
    nt = (((1,), (1,)), ((), ()))
    n_blk = seq // Q_BLK
    lo1 = lax.broadcasted_iota(I32, (Q_BLK, LANES), 1) < HEAD_DIM
    ones = jnp.ones((Q_BLK, LANES), F32)

    zpad = jnp.zeros((NEAR_W, LANES), BF16)
    kpad[0:NEAR_W, :] = zpad
    v0pad[0:NEAR_W, :] = zpad
    v1pad[0:NEAR_W, :] = zpad

    def fill(i, c):
        s0 = pl.multiple_of(i * Q_BLK, Q_BLK)
        d0 = pl.multiple_of(i * Q_BLK + NEAR_W, Q_BLK)
        kpad[pl.ds(d0, Q_BLK), :] = k_ref[pl.ds(s0, Q_BLK), :].astype(BF16)
        vb = v_ref[pl.ds(s0, Q_BLK), :]
        v0pad[pl.ds(d0, Q_BLK), :] = jnp.where(lo1, vb, ones).astype(BF16)
        v1pad[pl.ds(d0, Q_BLK), :] = jnp.where(lo1, ones, vb).astype(BF16)
        return c
    lax.fori_loop(0, n_blk, fill, 0)

    nf = seq // far_s
    row = lax.broadcasted_iota(I32, (2 * nf, nf), 0) & (nf - 1)
    col = lax.broadcasted_iota(I32, (2 * nf, nf), 1)
    allowed = jnp.logical_and(col <= row, ((row - col) & (FAR_D // far_s - 1)) == 0)
    lof = lax.broadcasted_iota(I32, (nf, LANES), 1) < HEAD_DIM
    onef = jnp.ones((nf, LANES), F32)

    def far(r, c):
        sl = pl.ds(r, nf, stride=far_s)
        q2 = _stack_heads(q_ref[sl, :]).astype(BF16)
        kr = k_ref[sl, :].astype(BF16)
        vr = v_ref[sl, :]
        s = lax.dot_general(q2, kr, nt, preferred_element_type=F32)
        s = jnp.where(allowed, s, NEG)
        m = jnp.max(s, axis=1, keepdims=True)
        p = jnp.exp(s - m).astype(BF16)
        r0 = jnp.dot(p[:nf], jnp.where(lof, vr, onef).astype(BF16), preferred_element_type=F32)
        r1 = jnp.dot(p[nf:], jnp.where(lof, onef, vr).astype(BF16), preferred_element_type=F32)
        num, den, mx = _split_pv(r0, r1, m, nf)
        num3[sl, :] = num
        den3[sl, :] = den
        max3[sl, :] = mx
        return c
    lax.fori_loop(0, far_s, far, 0, unroll=far_unroll)

    nk = NEAR_W + nq
    kcol = lax.broadcasted_iota(I32, (1, nk), 1)

    def near(first_blocks):
        def body(i, c):
            s0 = pl.multiple_of(i * nq, nq)
            q2 = _stack_heads(q_ref[pl.ds(s0, nq), :]).astype(BF16)
            kw = kpad[pl.ds(s0, nk), :]
            s = lax.dot_general(q2, kw, nt, preferred_element_type=F32) + lmult_ref[...]
            if first_blocks:
                s = jnp.where(kcol >= NEAR_W - s0, s, NEG)
            m = jnp.max(s, axis=1, keepdims=True)
            p = jnp.exp(s - m).astype(BF16)
            r0 = jnp.dot(p[:nq], v0pad[pl.ds(s0, nk), :], preferred_element_type=F32)
            r1 = jnp.dot(p[nq:], v1pad[pl.ds(s0, nk), :], preferred_element_type=F32)
            num, den, mx = _split_pv(r0, r1, m, nq)
            mx3 = max3[pl.ds(s0, nq), :]
            mm = jnp.maximum(mx, mx3)
            a = jnp.exp(mx - mm)
            b = jnp.exp(mx3 - mm)
            o_ref[pl.ds(s0, nq), :] = ((num * a + num3[pl.ds(s0, nq), :] * b)
                                       / (den * a + den3[pl.ds(s0, nq), :] * b))
            return c
        return body

    n_first = NEAR_W // nq
    lax.fori_loop(0, n_first, near(True), 0, unroll=near_unroll)
    lax.fori_loop(n_first, seq // nq, near(False), 0, unroll=near_unroll)


def _near_table(nq):
    i = np.arange(nq)[:, None]
    kl = np.arange(NEAR_W + nq)[None, :]
    delta = i + NEAR_W - kl
    mult = np.zeros(delta.shape, np.float32)
    for w, d in PATTERNS[:2]:
        mult += ((delta >= 0) & (delta <= w) & (delta % d == 0)).astype(np.float32)
    lmult = np.where(mult > 0, np.log(np.maximum(mult, 1.0)), NEG).astype(np.float32)
    return np.tile(lmult, (2, 1))


def _attn_prompt(q, k, v, far_unroll=4, near_unroll=2, nq=2 * Q_BLK, far_s=FAR_D):
    b, s, _ = q.shape
    lmult = _near_table(nq)
    blk = pl.BlockSpec((None, s, LANES), lambda bi, hp: (bi, 0, hp))
    tab = pl.BlockSpec((2 * nq, NEAR_W + nq), lambda bi, hp: (0, 0))
    return pl.pallas_call(
        functools.partial(_attn_prompt_kernel, seq=s, far_unroll=far_unroll, near_unroll=near_unroll,
                          nq=nq, far_s=far_s),
        grid=(b, ATT_W // LANES),
        in_specs=[blk, blk, blk, tab],
        out_specs=blk,
        out_shape=jax.ShapeDtypeStruct((b, s, ATT_W), F32),
        scratch_shapes=[pltpu.VMEM((s + NEAR_W, LANES), BF16)] * 3
        + [pltpu.VMEM((s, LANES), F32)] * 3,
        compiler_params=_cparams(2),
        name="attn_prompt",
    )(q, k, v, jnp.asarray(lmult))


def _attn_sample_kernel(q_ref, kt_ref, vt_ref, kn_ref, vn_ref, bias_ref, mult_ref, o_ref):
    nt = (((1,), (1,)), ((), ()))
    q = q_ref[...]
    bias = bias_ref[...]
    mult = mult_ref[...]
    n_c = kt_ref.shape[-1]
    kn = kn_ref[...]
    vn = vn_ref[...]
    zrows = jnp.zeros((LANES - 8, HEAD_DIM), F32)
    for h in range(N_HEADS):
        hs = slice(h * HEAD_DIM, (h + 1) * HEAD_DIM)
        qh = q[:, hs].astype(BF16)
        knh = jnp.concatenate([kn[:, hs], zrows], axis=0).astype(BF16)
        vnh = jnp.concatenate([vn[:, hs], zrows], axis=0).astype(BF16)
        s = jnp.concatenate(
            [jnp.dot(qh, kt_ref[h].astype(BF16), preferred_element_type=F32),
             lax.dot_general(qh, knh, nt, preferred_element_type=F32)], axis=1) + bias
        m = jnp.max(s, axis=1, keepdims=True)
        p = jnp.exp(s - m) * mult
        den = jnp.sum(p, axis=1, keepdims=True)
        pb = p.astype(BF16)
        num = (lax.dot_general(pb[:, :n_c], vt_ref[h].astype(BF16), nt, preferred_element_type=F32)
               + jnp.dot(pb[:, n_c:], vnh, preferred_element_type=F32))
        o_ref[:, hs] = num / den


def _sample_tables(n_cache, t_new):
    t = np.arange(8)[:, None] % t_new
    rho = np.arange(n_cache)[None, :]
    c_cache = _multiplicity(t + n_cache - rho)
    tp = np.arange(LANES)[None, :]
    c_new = np.where(tp < t_new, _multiplicity(t - tp), 0.0)
    mult = np.concatenate([c_cache, c_new], axis=1).astype(np.float32)
    bias = np.where(mult > 0, 0.0, NEG).astype(np.float32)
    return bias, mult


def _attn_sample(q8, kt, vt, kn8, vn8, t_new):
    b = q8.shape[0]
    n_c = kt.shape[-1]
    bias, mult = _sample_tables(n_c, t_new)
    cache = pl.BlockSpec((None, N_HEADS, HEAD_DIM, n_c), lambda i: (i, 0, 0, 0))
    row = pl.BlockSpec((None, 8, ATT_W), lambda i: (i, 0, 0))
    tab = pl.BlockSpec((8, n_c + LANES), lambda i: (0, 0))
    return pl.pallas_call(
        _attn_sample_kernel,
        grid=(b,),
        in_specs=[row, cache, cache, row, row, tab, tab],
        out_specs=row,
        out_shape=jax.ShapeDtypeStruct((b, 8, ATT_W), F32),
        compiler_params=_cparams(1),
        name="attn_sample",
    )(q8, kt, vt, kn8, vn8, jnp.asarray(bias), jnp.asarray(mult))


def _merge_kernel(att_ref, conv_ref, x_ref, gt1_ref, sh2_ref, sc2_ref, ga_ref, gc_ref, gf_ref,
                  wo_ref, wr_ref, wrl_ref, br_ref, cnt0_ref, tri_ref,
                  x1_ref, h2_ref, idx_ref, gate_ref, rank_ref, cnt_ref, cnt_sc, *, n_axes):
    first = pl.program_id(0) == 0
    if n_axes == 2:
        first = jnp.logical_and(first, pl.program_id(1) == 0)

    @pl.when(first)
    def _():
        cnt_sc[...] = cnt0_ref[...]

    an = _rmsnorm(att_ref[...], ga_ref[...]).astype(BF16)
    cn = _rmsnorm(conv_ref[...], gc_ref[...]).astype(BF16)
    mix = (jnp.dot(an, wo_ref[0:ATT_W, :], preferred_element_type=F32)
           + jnp.dot(cn, wo_ref[ATT_W:D_MODEL, :], preferred_element_type=F32))
    x1 = x_ref[...] + gt1_ref[...] * mix
    x1_ref[...] = x1
    h2 = _norm_mod(x1, gf_ref[...], sh2_ref[...], sc2_ref[...])
    _store_token_tiles(h2_ref, h2)
    hi = h2.astype(BF16)
    lo = (h2 - hi.astype(F32)).astype(BF16)
    logits = (jnp.dot(hi, wr_ref[...], preferred_element_type=F32)
              + (jnp.dot(hi, wrl_ref[...], preferred_element_type=F32)
                 + jnp.dot(lo, wr_ref[...], preferred_element_type=F32))) + br_ref[...]
    tm = logits.shape[0]
    lane = lax.broadcasted_iota(I32, (tm, LANES), 1)
    work = logits
    vals, idxs = [], []
    for _ in range(TOP_K):
        mx = jnp.max(work, axis=1, keepdims=True)
        ix = jnp.min(jnp.where(work == mx, lane, LANES), axis=1, keepdims=True)
        vals.append(mx)
        idxs.append(ix)
        work = jnp.where(lane == ix, 3.0 * NEG, work)
    es = [jnp.exp(v - vals[0]) for v in vals]
    den = es[0] + es[1] + es[2] + es[3]
    onehot = jnp.zeros((tm, LANES), F32)
    for ix in idxs:
        onehot = onehot + (lane == ix).astype(F32)
    before = jnp.dot(tri_ref[...], onehot.astype(BF16), preferred_element_type=F32) + cnt_sc[0:1, :]
    idx_o = jnp.zeros((tm, LANES), I32)
    gate_o = jnp.zeros((tm, LANES), F32)
    rank_o = jnp.zeros((tm, LANES), F32)
    for t in range(TOP_K):
        rk = jnp.sum(jnp.where(lane == idxs[t], before, 0.0), axis=1, keepdims=True)
        idx_o = jnp.where(lane == t, idxs[t], idx_o)
        gate_o = jnp.where(lane == t, es[t] / den, gate_o)
        rank_o = jnp.where(lane == t, rk, rank_o)
    idx_ref[...] = idx_o
    gate_ref[...] = gate_o
    rank_ref[...] = rank_o.astype(I32)
    cnt_sc[...] = cnt_sc[...] + jnp.sum(onehot, axis=0, keepdims=True)
    cnt_ref[...] = cnt_sc[...]


def _merge(att, conv, x, gt1, sh2, sc2, ga, gc, gf, wo_bf, wr_pad, br_pad, cnt0, tm):
    d = D_MODEL
    wr_hi = wr_pad.astype(BF16)
    wr_lo = (wr_pad - wr_hi.astype(F32)).astype(BF16)
    tri = jnp.asarray(np.tril(np.ones((tm, tm), np.float32), -1), BF16)
    if att.ndim == 3:
        b, s, _ = att.shape
        grid = (b, s // tm)
        row = lambda w: pl.BlockSpec((None, tm, w), lambda bi, j: (bi, j, 0))
        mod = pl.BlockSpec((None, 1, d), lambda bi, j: (bi, 0, 0))
        const = lambda shape: pl.BlockSpec(shape, lambda bi, j: (0, 0))
        lead = (b, s)
        tiles = pl.BlockSpec((None, tm * TOK_ROWS, LANES), lambda bi, j: (bi, j, 0))
        tiles_shape = (b, s * TOK_ROWS, LANES)
    else:
        n = att.shape[0]
        grid = (n // tm,)
        row = lambda w: pl.BlockSpec((tm, w), lambda i: (i, 0))
        mod = row(d)
        const = lambda shape: pl.BlockSpec(shape, lambda i: (0, 0))
        lead = (n,)
        tiles = pl.BlockSpec((tm * TOK_ROWS, LANES), lambda i: (i, 0))
        tiles_shape = (n * TOK_ROWS, LANES)
    return pl.pallas_call(
        functools.partial(_merge_kernel, n_axes=len(grid)),
        grid=grid,
        in_specs=[row(ATT_W), row(CONV_W), row(d), mod, mod, mod,
                  const((1, ATT_W)), const((1, CONV_W)), const((1, d)),
                  const((d, d)), const((d, LANES)), const((d, LANES)), const((1, LANES)),
                  const((8, LANES)), const((tm, tm))],
        out_specs=[row(d), tiles, row(LANES), row(LANES), row(LANES), const((8, LANES))],
        out_shape=[jax.ShapeDtypeStruct(lead + (d,), F32),
                   jax.ShapeDtypeStruct(tiles_shape, F32),
                   jax.ShapeDtypeStruct(lead + (LANES,), I32),
                   jax.ShapeDtypeStruct(lead + (LANES,), F32),
                   jax.ShapeDtypeStruct(lead + (LANES,), I32),
                   jax.ShapeDtypeStruct((8, LANES), F32)],
        scratch_shapes=[pltpu.VMEM((8, LANES), F32)],
        compiler_params=_cparams(len(grid)),
        name="merge_route",
    )(att, conv, x, gt1, sh2, sc2, ga, gc, gf, wo_bf, wr_hi, wr_lo, br_pad, cnt0, tri)


def _tok(ref, n, count=1):
    return ref.at[pl.ds(pl.multiple_of(n * TOK_ROWS, TOK_ROWS), count * TOK_ROWS)]


ROW_DMA_UNROLL = 4


def _start_rows(dest_ref, tm, make):
    def start(n, c):
        for t in range(TOP_K):
            make(n, t, dest_ref[0, n * TOP_K + t]).start(priority=t % 2)
        return c
    lax.fori_loop(0, tm, start, 0, unroll=ROW_DMA_UNROLL)


def _wait_rows(dest_ref, tm, make):
    def wait(n, c):
        for t in range(TOP_K):
            make(n, t, dest_ref[0, n * TOP_K + t]).wait()
        return c
    lax.fori_loop(0, tm, wait, 0, unroll=2 * ROW_DMA_UNROLL)


def _row_copies(dest_ref, tm, make):
    _start_rows(dest_ref, tm, make)
    _wait_rows(dest_ref, tm, make)


def _dispatch_kernel(ps_ref, pn_ref, t0_ref, dest_ref, h_ref, dest_s_ref, hs_ref, xs_ref,
                     zbuf, sem, *, tm, n_s, row_tm, n_tiles):
    def make(n, t, d):
        return pltpu.make_async_copy(_tok(h_ref, n), _tok(xs_ref, d), sem)
    _row_copies(dest_ref, tm, make)

    @pl.when(pl.program_id(0) == pl.num_programs(0) - 1)
    def _():
        def make_s(n, t, d):
            return pltpu.make_async_copy(_tok(hs_ref, n), _tok(xs_ref, d), sem)
        _row_copies(dest_s_ref, n_s, make_s)
        _zero_fill(ps_ref, pn_ref, t0_ref, xs_ref, zbuf, sem, row_tm, n_tiles)


def _dispatch(h2, dest, h2_s, dest_s, pad, n_rows, tm=4 * TOK_TM, row_tm=MOE_TM):
    n = h2.shape[0] // TOK_ROWS
    n_s = h2_s.shape[0] // TOK_ROWS
    nt = n // tm
    return pl.pallas_call(
        functools.partial(_dispatch_kernel, tm=tm, n_s=n_s, row_tm=row_tm, n_tiles=n_rows // row_tm),
        grid_spec=pltpu.PrefetchScalarGridSpec(
            num_scalar_prefetch=3,
            grid=(nt,),
            in_specs=[pl.BlockSpec((None, 1, tm * TOP_K), lambda i, *_: (i, 0, 0), memory_space=pltpu.SMEM),
                      pl.BlockSpec((tm * TOK_ROWS, LANES), lambda i, *_: (i, 0)),
                      pl.BlockSpec((1, n_s * TOP_K), lambda i, *_: (0, 0), memory_space=pltpu.SMEM),
                      pl.BlockSpec((n_s * TOK_ROWS, LANES), lambda i, *_: (0, 0))],
            out_specs=pl.BlockSpec(memory_space=pl.ANY),
            scratch_shapes=[pltpu.VMEM((row_tm * TOK_ROWS, LANES), F32), pltpu.SemaphoreType.DMA(())],
        ),
        out_shape=jax.ShapeDtypeStruct((n_rows * TOK_ROWS, LANES), F32),
        compiler_params=_cparams(1),
        name="moe_dispatch",
    )(*pad, dest.reshape(nt, 1, tm * TOP_K), h2, dest_s.reshape(1, n_s * TOP_K), h2_s)


def _zero_fill(ps_ref, pn_ref, t0_ref, xs_ref, zbuf, sem, tm, n_tiles):
    zbuf[...] = jnp.zeros_like(zbuf)
    bits = [tm >> (k + 1) for k in range(tm.bit_length() - 1)]

    def segments(act):
        def seg(e, c):
            off = ps_ref[e]
            ln = pn_ref[e]
            for bit in bits:
                @pl.when((ln & bit) != 0)
                def _(off=off, bit=bit):
                    act(pltpu.make_async_copy(_tok(zbuf, 0, bit), _tok(xs_ref, off, bit), sem))
                off = off + (ln & bit)
            return c
        lax.fori_loop(0, N_EXPERTS, seg, 0)

        def tile(i, c):
            act(pltpu.make_async_copy(zbuf, _tok(xs_ref, i * tm, tm), sem))
            return c
        lax.fori_loop(t0_ref[0], n_tiles, tile, 0)

    segments(lambda cp: cp.start())
    segments(lambda cp: cp.wait())


def _expert_kernel(te_ref, tv_ref, nx_ref, sl_ref, xs_ref, wg_hbm, bg_ref, wu_hbm, bu_ref,
                   wd_hbm, bd_ref, ys_ref, wbuf, wg_bf, wu_bf, wd_bf, sems, *, tm):
    i = pl.program_id(0)
    e = te_ref[i]
    slot = sl_ref[i]
    new_expert = jnp.logical_or(i == 0, e != te_ref[jnp.maximum(i - 1, 0)])

    def fetch(expert, s):
        return [pltpu.make_async_copy(w.at[expert], wbuf.at[s, k], sems.at[s, k])
                for k, w in enumerate((wg_hbm, wu_hbm, wd_hbm))]

    @pl.when(i == 0)
    def _():
        for cp in fetch(e, slot):
            cp.start()

    @pl.when(new_expert)
    def _():
        for cp in fetch(e, slot):
            cp.wait()

        @pl.when(nx_ref[i] >= 0)
        def _():
            for cp in fetch(nx_ref[i], 1 - slot):
                cp.start()
        wg_bf[...] = wbuf[slot, 0].astype(BF16)
        wu_bf[...] = wbuf[slot, 1].astype(BF16)
        wd_bf[...] = wbuf[slot, 2].astype(BF16)

    @pl.when(tv_ref[i] > 0)
    def _():
        x = _load_token_tiles(xs_ref, tm).astype(BF16)
        g = jnp.dot(x, wg_bf[...], preferred_element_type=F32) + bg_ref[...]
        u = jnp.dot(x, wu_bf[...], preferred_element_type=F32) + bu_ref[...]
        g = jnp.minimum(g, SWIGLU_LIMIT)
        u = jnp.clip(u, -SWIGLU_LIMIT, SWIGLU_LIMIT)
        act = (u + 1.0) * g * (1.0 / (1.0 + jnp.exp(-SWIGLU_ALPHA * g)))
        y = jnp.dot(act.astype(BF16), wd_bf[...], preferred_element_type=F32) + bd_ref[...]
        _store_token_tiles(ys_ref, y)

    @pl.when(tv_ref[i] == 0)
    def _():
        ys_ref[...] = jnp.zeros_like(ys_ref)


def _experts(xs, plan, wg, bg, wu, bu, wd, bd, tm=MOE_TM):
    n_tiles = xs.shape[0] // (tm * TOK_ROWS)
    d, f = wg.shape[-2:]
    assert d == f == D_MODEL
    b_spec = lambda n: pl.BlockSpec((None, 1, n), lambda i, te, *_: (te[i], 0, 0))
    rows = pl.BlockSpec((tm * TOK_ROWS, LANES), lambda i, *_: (i, 0))
    hbm = pl.BlockSpec(memory_space=pl.ANY)
    return pl.pallas_call(
        functools.partial(_expert_kernel, tm=tm),
        grid_spec=pltpu.PrefetchScalarGridSpec(
            num_scalar_prefetch=4,
            grid=(n_tiles,),
            in_specs=[rows, hbm, b_spec(f), hbm, b_spec(f), hbm, b_spec(d)],
            out_specs=rows,
            scratch_shapes=[pltpu.VMEM((2, 3, d, f), F32),
                            pltpu.VMEM((d, f), BF16), pltpu.VMEM((d, f), BF16), pltpu.VMEM((f, d), BF16),
                            pltpu.SemaphoreType.DMA((2, 3))],
        ),
        out_shape=jax.ShapeDtypeStruct(xs.shape, F32),
        compiler_params=_cparams(1),
        name="moe_experts",
    )(*plan, xs, wg, bg.reshape(N_EXPERTS, 1, f), wu, bu.reshape(N_EXPERTS, 1, f),
      wd, bd.reshape(N_EXPERTS, 1, d))


def _combine_kernel(dcur_ref, dnxt_ref, ys_ref, x1_ref, gate_ref, gt2_ref, gfin_ref, o_ref,
                    buf, sems, *, tm):
    i = pl.program_id(0)
    cur = i & 1
    slot_toks = TOP_K * tm

    def make(slot):
        def f(n, t, d):
            return pltpu.make_async_copy(_tok(ys_ref, d), _tok(buf, slot * slot_toks + t * tm + n),
                                         sems.at[slot])
        return f

    @pl.when(i == 0)
    def _():
        _start_rows(dcur_ref, tm, make(cur))

    @pl.when(i + 1 < pl.num_programs(0))
    def _():
        _start_rows(dnxt_ref, tm, make(1 - cur))

    _wait_rows(dcur_ref, tm, make(cur))
    base = pl.multiple_of(cur * slot_toks * TOK_ROWS, slot_toks * TOK_ROWS)
    gate = gate_ref[...]
    y = gate[:, 0:1] * _load_token_tiles(buf, tm, base=base)
    for t in range(1, TOP_K):
        y = y + gate[:, t:t + 1] * _load_token_tiles(buf, tm, base=base + t * tm * TOK_ROWS)
    x2 = x1_ref[...] + gt2_ref[...] * y
    o_ref[...] = _rmsnorm(x2, gfin_ref[...])


def _combine(ys, dest, x1, gate, gt2, gfin, seq, tm=TOK_TM):
    d = D_MODEL
    n = x1.shape[0]
    tm = min(tm, n)
    nt = n // tm
    row = lambda w: pl.BlockSpec((tm, w), lambda i: (i, 0))
    if seq:
        mod = pl.BlockSpec((None, 1, d), lambda i: (i // (seq // tm), 0, 0))
    else:
        mod = row(d)
    dspec = lambda fn: pl.BlockSpec((None, 1, tm * TOP_K), lambda i: (fn(i), 0, 0), memory_space=pltpu.SMEM)
    dest3 = dest.reshape(nt, 1, tm * TOP_K)
    return pl.pallas_call(
        functools.partial(_combine_kernel, tm=tm),
        grid=(nt,),
        in_specs=[dspec(lambda i: i), dspec(lambda i: jnp.minimum(i + 1, nt - 1)),
                  pl.BlockSpec(memory_space=pl.ANY), row(d), row(LANES), mod,
                  pl.BlockSpec((1, d), lambda i: (0, 0))],
        out_specs=row(d),
        out_shape=jax.ShapeDtypeStruct(x1.shape, F32),
        scratch_shapes=[pltpu.VMEM((2 * TOP_K * tm * TOK_ROWS, LANES), F32),
                        pltpu.SemaphoreType.DMA((2,))],
        compiler_params=_cparams(1),
        name="moe_combine",
    )(dest3, dest3, ys, x1, gate, gt2, gfin)


def _routing_plan(counts, n_pairs, tm=MOE_TM):
    pc = (counts + tm - 1) // tm * tm
    pend = jnp.cumsum(pc)
    pstart = pend - pc
    n_rows = -(-(n_pairs + N_EXPERTS * (tm - 1)) // tm) * tm
    n_tiles = n_rows // tm
    tile_row = jnp.arange(n_tiles, dtype=I32) * tm
    last_used = jnp.max(jnp.where(pc > 0, jnp.arange(N_EXPERTS, dtype=I32), 0))
    tile_e = jnp.minimum(jnp.sum((tile_row[:, None] >= pend[None, :]).astype(I32), axis=1), last_used)
    tile_valid = (tile_row < pend[-1]).astype(I32)
    ids = jnp.arange(N_EXPERTS, dtype=I32)
    used = pc > 0
    slot_e = (jnp.cumsum(used.astype(I32)) - 1) & 1
    later = jnp.where(used[None, :] & (ids[None, :] > ids[:, None]), ids[None, :], N_EXPERTS)
    next_e = jnp.min(later, axis=1)
    next_e = jnp.where(next_e == N_EXPERTS, -1, next_e)
    pick = (tile_e[:, None] == ids[None, :]).astype(I32)
    plan = (tile_e, tile_valid, jnp.sum(pick * next_e[None, :], axis=1),
            jnp.sum(pick * slot_e[None, :], axis=1))
    pad = ((pstart + counts).astype(I32), (pc - counts).astype(I32),
           (pend[-1:] // tm).astype(I32))
    return pstart.astype(I32), plan, n_rows, pad


def _rope_tables(pos):
    half = HEAD_DIM // 2
    inv = ROPE_THETA ** (-np.arange(half, dtype=np.float64) / half)
    ang = np.asarray(pos, np.float64)[:, None] * inv[None, :]
    cos = np.cos(ang)
    sin = np.sin(ang)
    cosf = np.concatenate([cos, cos, cos, cos], axis=1).astype(np.float32)
    sinf = np.concatenate([-sin, sin, -sin, sin], axis=1).astype(np.float32)
    return jnp.asarray(cosf), jnp.asarray(sinf)


def kernel(x_prompt, x_sample, cache_k, cache_v, state_conv, c_prompt, c_sample, w_ada, b_ada,
           g_norm_mix, w_in, conv_w, g_attn_out, g_conv_out, w_out, g_norm_ffn, w_router, b_router,
           w_gate, b_gate, w_up, b_up, w_down, b_down, g_final):
    depth = w_in.shape[0]
    assert depth == 1, "single-layer trunk"
    bp, sp, d = x_prompt.shape
    bs, ts, _ = x_sample.shape
    ns = bs * ts
    n_cache = cache_k.shape[2]
    l = 0

    mods = _modulations(jnp.concatenate([c_prompt, c_sample], axis=0), w_ada[l], b_ada[l])
    mp = [m.reshape(bp, 1, d) for m in jnp.split(mods[:bp], 6, axis=-1)]
    ms = [jnp.repeat(m, ts, axis=0) for m in jnp.split(mods[bp:], 6, axis=-1)]

    w_in_bf = w_in[l].astype(BF16)
    w_out_bf = w_out[l].astype(BF16)
    wr_pad = jnp.pad(w_router[l], ((0, 0), (0, LANES - N_EXPERTS)))
    br_pad = jnp.pad(b_router[l].reshape(1, N_EXPERTS), ((0, 0), (0, LANES - N_EXPERTS)),
                     constant_values=NEG)
    g_mix = g_norm_mix[l].reshape(1, d)
    g_ffn = g_norm_ffn[l].reshape(1, d)
    g_att = g_attn_out[l].reshape(1, ATT_W)
    g_cnv = g_conv_out[l].reshape(1, CONV_W)
    g_fin = g_final.reshape(1, d)

    cos_p, sin_p = _rope_tables(np.arange(sp))
    q_p, k_p, v_p, conv_p, tail_p = _inproj_prompt(x_prompt, mp[0], mp[1], g_mix, w_in_bf,
                                                   cos_p, sin_p, conv_w[l])
    att_p = _attn_prompt(q_p, k_p, v_p)

    xs_rows = x_sample.reshape(ns, d)
    cos_s, sin_s = _rope_tables(np.tile(PAST_LEN + np.arange(ts), bs))
    st = state_conv[l]
    zrow = jnp.zeros((bs, 1, CONV_W), F32)
    s1 = jnp.concatenate([st[:, 1:2], zrow, zrow, zrow], axis=1).reshape(ns, CONV_W)
    s2 = jnp.concatenate([st[:, 0:1], st[:, 1:2], zrow, zrow], axis=1).reshape(ns, CONV_W)
    q_s, k_s, v_s, conv_s, cu_s = _inproj_sample(xs_rows, ms[0], ms[1], g_mix, w_in_bf,
                                                 cos_s, sin_s, conv_w[l], s1, s2, ts)
    rows8 = lambda a: jnp.pad(a.reshape(bs, ts, ATT_W), ((0, 0), (0, 8 - ts), (0, 0)))
    kt = jnp.transpose(cache_k[l], (0, 2, 3, 1))
    vt = jnp.transpose(cache_v[l], (0, 2, 3, 1))
    att_s = _attn_sample(rows8(q_s), kt, vt, rows8(k_s), rows8(v_s), ts)[:, :ts].reshape(ns, ATT_W)

    cnt0 = jnp.zeros((8, LANES), F32)
    x1_p, h2_p, idx_p, gate_p, rank_p, cnt_p = _merge(
        att_p, conv_p, x_prompt, mp[2], mp[3], mp[4], g_att, g_cnv, g_ffn,
        w_out_bf, wr_pad, br_pad, cnt0, 2 * TOK_TM)
    x1_s, h2_s, idx_s, gate_s, rank_s, cnt_all = _merge(
        att_s, conv_s, xs_rows, ms[2], ms[3], ms[4], g_att, g_cnv, g_ffn,
        w_out_bf, wr_pad, br_pad, cnt_p, ns)

    n_tok = bp * sp + ns
    counts = cnt_all[0, :N_EXPERTS].astype(I32)
    pstart, plan, n_rows, pad = _routing_plan(counts, n_tok * TOP_K)
    ids = jnp.arange(N_EXPERTS, dtype=I32)
    slot_of = lambda idx, rank: jnp.sum(
        jnp.where(idx[..., :TOP_K, None] == ids, pstart, 0), axis=-1) + rank[..., :TOP_K]
    dest_p = slot_of(idx_p, rank_p).reshape(bp * sp, TOP_K)
    dest_s = slot_of(idx_s, rank_s)
    xs_sorted = _dispatch(h2_p.reshape(bp * sp * TOK_ROWS, LANES), dest_p, h2_s, dest_s, pad, n_rows)
    ys = _experts(xs_sorted, plan, w_gate[l], b_gate[l], w_up[l], b_up[l], w_down[l], b_down[l])
    y_prompt = _combine(ys, dest_p, x1_p.reshape(bp * sp, d), gate_p.reshape(bp * sp, LANES),
                        mp[5], g_fin, sp).reshape(bp, sp, d)
    y_sample = _combine(ys, dest_s, x1_s, gate_s, ms[5], g_fin, 0).reshape(bs, ts, d)

    heads = lambda a, b, s: a.reshape(1, b, s, N_HEADS, HEAD_DIM)
    keep = min(WINDOW_MAX, sp)
    return (y_prompt, y_sample,
            heads(k_p, bp, sp)[:, :, sp - keep:], heads(v_p, bp, sp)[:, :, sp - keep:],
            tail_p[:, 8 - (CONV_K - 1):][None],
            heads(k_s, bs, ts), heads(v_s, bs, ts),
            cu_s.reshape(bs, ts, CONV_W)[:, ts - (CONV_K - 1):][None])
```

```python
import functools

import jax
import jax.numpy as jnp
import numpy as np
from jax import lax
from jax.experimental import pallas as pl
from jax.experimental.pallas import tpu as pltpu

F32 = jnp.float32
BF16 = jnp.bfloat16
I32 = jnp.int32
HIGHEST = lax.Precision.HIGHEST

D_MODEL = 1024
HEAD_DIM = 64
N_HEADS = 12
ATT_W = N_HEADS * HEAD_DIM
CONV_W = D_MODEL - ATT_W
CONV_K = 3
PATTERNS = ((128, 1), (512, 4), (2048, 16))
WINDOW_MAX = 2048
PAST_LEN = 16384
ROPE_THETA = 10000.0
N_EXPERTS = 32
TOP_K = 4
SWIGLU_ALPHA = 1.702
SWIGLU_LIMIT = 7.0
NORM_EPS = 1e-6
NEG = -1e30
IN_W = 3 * ATT_W + 3 * CONV_W

LANES = 128
Q_BLK = 128
NEAR_W = 512
FAR_D = 16
MOE_TM = 256
TOK_TM = 256
VMEM_LIMIT = 56 * 1024 * 1024
TOK_ROWS = D_MODEL // LANES


def _cparams(n_axes, vmem=VMEM_LIMIT):
    return pltpu.CompilerParams(dimension_semantics=("arbitrary",) * n_axes,
                                vmem_limit_bytes=vmem)


def _multiplicity(delta):
    delta = np.asarray(delta)
    c = np.zeros(delta.shape, np.float32)
    for w, d in PATTERNS:
        c += ((delta >= 0) & (delta <= w) & (delta % d == 0)).astype(np.float32)
    return c


def _ada_kernel(c_ref, w_ref, b_ref, o_ref):
    c = c_ref[...]
    s = c / (1.0 + jnp.exp(-c))
    o_ref[...] = jnp.dot(s, w_ref[...], precision=HIGHEST,
                         preferred_element_type=F32) + b_ref[...]


def _modulations(c_all, w_ada, b_ada):
    r, d = c_all.shape
    n = w_ada.shape[1]
    tn = 1536
    return pl.pallas_call(
        _ada_kernel,
        grid=(n // tn,),
        in_specs=[pl.BlockSpec((r, d), lambda j: (0, 0)),
                  pl.BlockSpec((d, tn), lambda j: (0, j)),
                  pl.BlockSpec((1, tn), lambda j: (0, j))],
        out_specs=pl.BlockSpec((r, tn), lambda j: (0, j)),
        out_shape=jax.ShapeDtypeStruct((r, n), F32),
        compiler_params=_cparams(1),
        name="ada_modulation",
    )(c_all, w_ada, b_ada.reshape(1, n))


def _norm_mod(x, g, shift, scale):
    ms = jnp.mean(x * x, axis=-1, keepdims=True)
    return (x * lax.rsqrt(ms + NORM_EPS) * g) * (1.0 + scale) + shift


def _rmsnorm(x, g):
    ms = jnp.mean(x * x, axis=-1, keepdims=True)
    return x * lax.rsqrt(ms + NORM_EPS) * g


def _store_token_tiles(ref, x, base=0):
    tm = x.shape[0]
    for c in range(TOK_ROWS):
        ref[pl.ds(base + c, tm, stride=TOK_ROWS), :] = x[:, c * LANES:(c + 1) * LANES]


def _load_token_tiles(ref, tm, base=0):
    return jnp.concatenate([ref[pl.ds(base + c, tm, stride=TOK_ROWS), :] for c in range(TOK_ROWS)],
                           axis=1)


def _swap_halves(xc):
    lane = lax.broadcasted_iota(I32, xc.shape, 1)
    first = (lane & (HEAD_DIM - 1)) < HEAD_DIM // 2
    return jnp.where(first, pltpu.roll(xc, LANES - HEAD_DIM // 2, 1),
                     pltpu.roll(xc, HEAD_DIM // 2, 1))


def _rope(x, cosf, sinf):
    outs = []
    for c in range(x.shape[1] // LANES):
        xc = x[:, c * LANES:(c + 1) * LANES]
        outs.append(xc * cosf + _swap_halves(xc) * sinf)
    return jnp.concatenate(outs, axis=1)


def _inproj_kernel(*refs, tm, sample, seq_per_batch):
    if sample:
        (x_ref, sh_ref, sc_ref, g_ref, w_ref, cos_ref, sin_ref, cw_ref, s1_ref, s2_ref,
         q_ref, k_ref, v_ref, conv_ref, cu_ref, cu_ext) = refs
    else:
        (x_ref, sh_ref, sc_ref, g_ref, w_ref, cos_ref, sin_ref, cw_ref,
         q_ref, k_ref, v_ref, conv_ref, tail_ref, cu_ext) = refs
    h = _norm_mod(x_ref[...], g_ref[...], sh_ref[...], sc_ref[...])
    z = jnp.dot(h.astype(BF16), w_ref[...], preferred_element_type=F32)
    cosf = cos_ref[...]
    sinf = sin_ref[...]
    q_ref[...] = _rope(z[:, 0:ATT_W], cosf, sinf) * (HEAD_DIM ** -0.5)
    k_ref[...] = _rope(z[:, ATT_W:2 * ATT_W], cosf, sinf)
    v_ref[...] = z[:, 2 * ATT_W:3 * ATT_W]
    o = 3 * ATT_W
    gb = z[:, o:o + CONV_W]
    cu = z[:, o + CONV_W:o + 2 * CONV_W] * z[:, o + 2 * CONV_W:o + 3 * CONV_W]
    if sample:
        cu_ext[0:8, :] = jnp.zeros((8, CONV_W), F32)
    else:
        @pl.when(pl.program_id(1) == 0)
        def _():
            cu_ext[0:8, :] = jnp.zeros((8, CONV_W), F32)
    cu_ext[8:8 + tm, :] = cu
    p1 = cu_ext[7:7 + tm, :]
    p2 = cu_ext[6:6 + tm, :]
    if sample:
        t = lax.broadcasted_iota(I32, (tm, CONV_W), 0) % seq_per_batch
        p1 = jnp.where(t >= 1, p1, 0.0) + s1_ref[...]
        p2 = jnp.where(t >= 2, p2, 0.0) + s2_ref[...]
        cu_ref[...] = cu
    cw = cw_ref[...]
    conv_ref[...] = gb * (cw[0:1, :] * p2 + cw[1:2, :] * p1 + cw[2:3, :] * cu)
    if not sample:
        tail = cu_ext[tm:tm + 8, :]
        tail_ref[...] = tail
        cu_ext[0:8, :] = tail


def _inproj_prompt(x, shift, scale, g, w_bf, cosf, sinf, conv_w, tm=512):
    b, s, d = x.shape
    row = lambda bi, j: (bi, j, 0)
    per_b = lambda bi, j: (bi, 0, 0)
    const = lambda bi, j: (0, 0)
    outs = pl.pallas_call(
        functools.partial(_inproj_kernel, tm=tm, sample=False, seq_per_batch=s),
        grid=(b, s // tm),
        in_specs=[pl.BlockSpec((None, tm, d), row),
                  pl.BlockSpec((None, 1, d), per_b),
                  pl.BlockSpec((None, 1, d), per_b),
                  pl.BlockSpec((1, d), const),
                  pl.BlockSpec((d, IN_W), const),
                  pl.BlockSpec((tm, LANES), lambda bi, j: (j, 0)),
                  pl.BlockSpec((tm, LANES), lambda bi, j: (j, 0)),
                  pl.BlockSpec((CONV_K, CONV_W), const)],
        out_specs=[pl.BlockSpec((None, tm, ATT_W), row),
                   pl.BlockSpec((None, tm, ATT_W), row),
                   pl.BlockSpec((None, tm, ATT_W), row),
                   pl.BlockSpec((None, tm, CONV_W), row),
                   pl.BlockSpec((None, 8, CONV_W), per_b)],
        out_shape=[jax.ShapeDtypeStruct((b, s, ATT_W), F32),
                   jax.ShapeDtypeStruct((b, s, ATT_W), F32),
                   jax.ShapeDtypeStruct((b, s, ATT_W), F32),
                   jax.ShapeDtypeStruct((b, s, CONV_W), F32),
                   jax.ShapeDtypeStruct((b, 8, CONV_W), F32)],
        scratch_shapes=[pltpu.VMEM((tm + 8, CONV_W), F32)],
        compiler_params=_cparams(2),
        name="inproj_prompt",
    )(x, shift, scale, g, w_bf, cosf, sinf, conv_w)
    return outs


def _inproj_sample(x, shift, scale, g, w_bf, cosf, sinf, conv_w, s1, s2, seq_per_batch):
    n, d = x.shape
    full = lambda shape: pl.BlockSpec(shape, lambda i: (0, 0))
    outs = pl.pallas_call(
        functools.partial(_inproj_kernel, tm=n, sample=True, seq_per_batch=seq_per_batch),
        grid=(1,),
        in_specs=[full((n, d)), full((n, d)), full((n, d)), full((1, d)), full((d, IN_W)),
                  full((n, LANES)), full((n, LANES)), full((CONV_K, CONV_W)),
                  full((n, CONV_W)), full((n, CONV_W))],
        out_specs=[full((n, ATT_W)), full((n, ATT_W)), full((n, ATT_W)),
                   full((n, CONV_W)), full((n, CONV_W))],
        out_shape=[jax.ShapeDtypeStruct((n, ATT_W), F32)] * 3
        + [jax.ShapeDtypeStruct((n, CONV_W), F32)] * 2,
        scratch_shapes=[pltpu.VMEM((n + 8, CONV_W), F32)],
        compiler_params=_cparams(1),
        name="inproj_sample",
    )(x, shift, scale, g, w_bf, cosf, sinf, conv_w, s1, s2)
    return outs


def _stack_heads(x):
    lo = lax.broadcasted_iota(I32, x.shape, 1) < HEAD_DIM
    zero = jnp.zeros_like(x)
    return jnp.concatenate([jnp.where(lo, x, zero), jnp.where(lo, zero, x)], axis=0)


def _split_pv(r0, r1, m, rows):
    lo = lax.broadcasted_iota(I32, r0.shape, 1) < HEAD_DIM
    num = jnp.where(lo, r0, r1)
    den = jnp.where(lo, pltpu.roll(r0, HEAD_DIM, 1), pltpu.roll(r1, HEAD_DIM, 1))
    mx = jnp.where(lo, jnp.broadcast_to(m[:rows], r0.shape), jnp.broadcast_to(m[rows:], r0.shape))
    return num, den, mx


def _attn_prompt_kernel(q_ref, k_ref, v_ref, lmult_ref, o_ref,
                        kpad, v0pad, v1pad, num3, den3, max3,
                        *, seq, far_unroll, near_unroll, nq, far_s):
    nt = (((1,), (1,)), ((), ()))
    n_blk = seq // Q_BLK
    lo1 = lax.broadcasted_iota(I32, (Q_BLK, LANES), 1) < HEAD_DIM
    ones = jnp.ones((Q_BLK, LANES), F32)

    zpad = jnp.zeros((NEAR_W, LANES), BF16)
    kpad[0:NEAR_W, :] = zpad
    v0pad[0:NEAR_W, :] = zpad
    v1pad[0:NEAR_W, :] = zpad

    def fill(i, c):
        s0 = pl.multiple_of(i * Q_BLK, Q_BLK)
        d0 = pl.multiple_of(i * Q_BLK + NEAR_W, Q_BLK)
        kpad[pl.ds(d0, Q_BLK), :] = k_ref[pl.ds(s0, Q_BLK), :].astype(BF16)
        vb = v_ref[pl.ds(s0, Q_BLK), :]
        v0pad[pl.ds(d0, Q_BLK), :] = jnp.where(lo1, vb, ones).astype(BF16)
        v1pad[pl.ds(d0, Q_BLK), :] = jnp.where(lo1, ones, vb).astype(BF16)
        return c
    lax.fori_loop(0, n_blk, fill, 0)

    nf = seq // far_s
    row = lax.broadcasted_iota(I32, (2 * nf, nf), 0) & (nf - 1)
    col = lax.broadcasted_iota(I32, (2 * nf, nf), 1)
    allowed = jnp.logical_and(col <= row, ((row - col) & (FAR_D // far_s - 1)) == 0)
    lof = lax.broadcasted_iota(I32, (nf, LANES), 1) < HEAD_DIM
    onef = jnp.ones((nf, LANES), F32)

    def far(r, c):
        sl = pl.ds(r, nf, stride=far_s)
        q2 = _stack_heads(q_ref[sl, :]).astype(BF16)
        kr = k_ref[sl, :].astype(BF16)
        vr = v_ref[sl, :]
        s = lax.dot_general(q2, kr, nt, preferred_element_type=F32)
        s = jnp.where(allowed, s, NEG)
        m = jnp.max(s, axis=1, keepdims=True)
        p = jnp.exp(s - m).astype(BF16)
        r0 = jnp.dot(p[:nf], jnp.where(lof, vr, onef).astype(BF16), preferred_element_type=F32)
        r1 = jnp.dot(p[nf:], jnp.where(lof, onef, vr).astype(BF16), preferred_element_type=F32)
        num, den, mx = _split_pv(r0, r1, m, nf)
        num3[sl, :] = num
        den3[sl, :] = den
        max3[sl, :] = mx
        return c
    lax.fori_loop(0, far_s, far, 0, unroll=far_unroll)

    nk = NEAR_W + nq
    kcol = lax.broadcasted_iota(I32, (1, nk), 1)

    def near(first_blocks):
        def body(i, c):
            s0 = pl.multiple_of(i * nq, nq)
            q2 = _stack_heads(q_ref[pl.ds(s0, nq), :]).astype(BF16)
            kw = kpad[pl.ds(s0, nk), :]
            s = lax.dot_general(q2, kw, nt, preferred_element_type=F32) + lmult_ref[...]
            if first_blocks:
                s = jnp.where(kcol >= NEAR_W - s0, s, NEG)
            m = jnp.max(s, axis=1, keepdims=True)
            p = jnp.exp(s - m).astype(BF16)
            r0 = jnp.dot(p[:nq], v0pad[pl.ds(s0, nk), :], preferred_element_type=F32)
            r1 = jnp.dot(p[nq:], v1pad[pl.ds(s0, nk), :], preferred_element_type=F32)
            num, den, mx = _split_pv(r0, r1, m, nq)
            mx3 = max3[pl.ds(s0, nq), :]
            mm = jnp.maximum(mx, mx3)
            a = jnp.exp(mx - mm)
            b = jnp.exp(mx3 - mm)
            o_ref[pl.ds(s0, nq), :] = ((num * a + num3[pl.ds(s0, nq), :] * b)
                                       / (den * a + den3[pl.ds(s0, nq), :] * b))
            return c
        return body

    n_first = NEAR_W // nq
    lax.fori_loop(0, n_first, near(True), 0, unroll=near_unroll)
    lax.fori_loop(n_first, seq // nq, near(False), 0, unroll=near_unroll)


def _near_table(nq):
    i = np.arange(nq)[:, None]
    kl = np.arange(NEAR_W + nq)[None, :]
    delta = i + NEAR_W - kl
    mult = np.zeros(delta.shape, np.float32)
    for w, d in PATTERNS[:2]:
        mult += ((delta >= 0) & (delta <= w) & (delta % d == 0)).astype(np.float32)
    lmult = np.where(mult > 0, np.log(np.maximum(mult, 1.0)), NEG).astype(np.float32)
    return np.tile(lmult, (2, 1))


def _attn_prompt(q, k, v, far_unroll=4, near_unroll=3, nq=2 * Q_BLK, far_s=FAR_D):
    b, s, _ = q.shape
    lmult = _near_table(nq)
    blk = pl.BlockSpec((None, s, LANES), lambda bi, hp: (bi, 0, hp))
    tab = pl.BlockSpec((2 * nq, NEAR_W + nq), lambda bi, hp: (0, 0))
    return pl.pallas_call(
        functools.partial(_attn_prompt_kernel, seq=s, far_unroll=far_unroll, near_unroll=near_unroll,
                          nq=nq, far_s=far_s),
        grid=(b, ATT_W // LANES),
        in_specs=[blk, blk, blk, tab],
        out_specs=blk,
        out_shape=jax.ShapeDtypeStruct((b, s, ATT_W), F32),
        scratch_shapes=[pltpu.VMEM((s + NEAR_W, LANES), BF16)] * 3
        + [pltpu.VMEM((s, LANES), F32)] * 3,
        compiler_params=_cparams(2),
        name="attn_prompt",
    )(q, k, v, jnp.asarray(lmult))


def _attn_sample_kernel(q_ref, kt_ref, vt_ref, kn_ref, vn_ref, bias_ref, mult_ref, o_ref):
    nt = (((1,), (1,)), ((), ()))
    q = q_ref[...]
    bias = bias_ref[...]
    mult = mult_ref[...]
    n_c = kt_ref.shape[-1]
    kn = kn_ref[...]
    vn = vn_ref[...]
    zrows = jnp.zeros((LANES - 8, HEAD_DIM), F32)
    for h in range(N_HEADS):
        hs = slice(h * HEAD_DIM, (h + 1) * HEAD_DIM)
        qh = q[:, hs].astype(BF16)
        knh = jnp.concatenate([kn[:, hs], zrows], axis=0).astype(BF16)
        vnh = jnp.concatenate([vn[:, hs], zrows], axis=0).astype(BF16)
        s = jnp.concatenate(
            [jnp.dot(qh, kt_ref[h].astype(BF16), preferred_element_type=F32),
             lax.dot_general(qh, knh, nt, preferred_element_type=F32)], axis=1) + bias
        m = jnp.max(s, axis=1, keepdims=True)
        p = jnp.exp(s - m) * mult
        den = jnp.sum(p, axis=1, keepdims=True)
        pb = p.astype(BF16)
        num = (lax.dot_general(pb[:, :n_c], vt_ref[h].astype(BF16), nt, preferred_element_type=F32)
               + jnp.dot(pb[:, n_c:], vnh, preferred_element_type=F32))
        o_ref[:, hs] = num / den


def _sample_tables(n_cache, t_new):
    t = np.arange(8)[:, None] % t_new
    rho = np.arange(n_cache)[None, :]
    c_cache = _multiplicity(t + n_cache - rho)
    tp = np.arange(LANES)[None, :]
    c_new = np.where(tp < t_new, _multiplicity(t - tp), 0.0)
    mult = np.concatenate([c_cache, c_new], axis=1).astype(np.float32)
    bias = np.where(mult > 0, 0.0, NEG).astype(np.float32)
    return bias, mult


def _attn_sample(q8, kt, vt, kn8, vn8, t_new):
    b = q8.shape[0]
    n_c = kt.shape[-1]
    bias, mult = _sample_tables(n_c, t_new)
    cache = pl.BlockSpec((None, N_HEADS, HEAD_DIM, n_c), lambda i: (i, 0, 0, 0))
    row = pl.BlockSpec((None, 8, ATT_W), lambda i: (i, 0, 0))
    tab = pl.BlockSpec((8, n_c + LANES), lambda i: (0, 0))
    return pl.pallas_call(
        _attn_sample_kernel,
        grid=(b,),
        in_specs=[row, cache, cache, row, row, tab, tab],
        out_specs=row,
        out_shape=jax.ShapeDtypeStruct((b, 8, ATT_W), F32),
        compiler_params=_cparams(1),
        name="attn_sample",
    )(q8, kt, vt, kn8, vn8, jnp.asarray(bias), jnp.asarray(mult))


def _merge_kernel(att_ref, conv_ref, x_ref, gt1_ref, sh2_ref, sc2_ref, ga_ref, gc_ref, gf_ref,
                  wo_ref, wr_ref, wrl_ref, br_ref, cnt0_ref, tri_ref,
                  x1_ref, h2_ref, idx_ref, gate_ref, rank_ref, cnt_ref, cnt_sc, *, n_axes):
    first = pl.program_id(0) == 0
    if n_axes == 2:
        first = jnp.logical_and(first, pl.program_id(1) == 0)

    @pl.when(first)
    def _():
        cnt_sc[...] = cnt0_ref[...]

    an = _rmsnorm(att_ref[...], ga_ref[...]).astype(BF16)
    cn = _rmsnorm(conv_ref[...], gc_ref[...]).astype(BF16)
    mix = (jnp.dot(an, wo_ref[0:ATT_W, :], preferred_element_type=F32)
           + jnp.dot(cn, wo_ref[ATT_W:D_MODEL, :], preferred_element_type=F32))
    x1 = x_ref[...] + gt1_ref[...] * mix
    x1_ref[...] = x1
    h2 = _norm_mod(x1, gf_ref[...], sh2_ref[...], sc2_ref[...])
    _store_token_tiles(h2_ref, h2)
    hi = h2.astype(BF16)
    lo = (h2 - hi.astype(F32)).astype(BF16)
    logits = (jnp.dot(hi, wr_ref[...], preferred_element_type=F32)
              + (jnp.dot(hi, wrl_ref[...], preferred_element_type=F32)
                 + jnp.dot(lo, wr_ref[...], preferred_element_type=F32))) + br_ref[...]
    tm = logits.shape[0]
    lane = lax.broadcasted_iota(I32, (tm, LANES), 1)
    work = logits
    vals, idxs = [], []
    for _ in range(TOP_K):
        mx = jnp.max(work, axis=1, keepdims=True)
        ix = jnp.min(jnp.where(work == mx, lane, LANES), axis=1, keepdims=True)
        vals.append(mx)
        idxs.append(ix)
        work = jnp.where(lane == ix, 3.0 * NEG, work)
    es = [jnp.exp(v - vals[0]) for v in vals]
    den = es[0] + es[1] + es[2] + es[3]
    onehot = jnp.zeros((tm, LANES), F32)
    for ix in idxs:
        onehot = onehot + (lane == ix).astype(F32)
    before = jnp.dot(tri_ref[...], onehot.astype(BF16), preferred_element_type=F32) + cnt_sc[0:1, :]
    idx_o = jnp.zeros((tm, LANES), I32)
    gate_o = jnp.zeros((tm, LANES), F32)
    rank_o = jnp.zeros((tm, LANES), F32)
    for t in range(TOP_K):
        rk = jnp.sum(jnp.where(lane == idxs[t], before, 0.0), axis=1, keepdims=True)
        idx_o = jnp.where(lane == t, idxs[t], idx_o)
        gate_o = jnp.where(lane == t, es[t] / den, gate_o)
        rank_o = jnp.where(lane == t, rk, rank_o)
    idx_ref[...] = idx_o
    gate_ref[...] = gate_o
    rank_ref[...] = rank_o.astype(I32)
    cnt_sc[...] = cnt_sc[...] + jnp.sum(onehot, axis=0, keepdims=True)
    cnt_ref[...] = cnt_sc[...]


def _merge(att, conv, x, gt1, sh2, sc2, ga, gc, gf, wo_bf, wr_pad, br_pad, cnt0, tm):
    d = D_MODEL
    wr_hi = wr_pad.astype(BF16)
    wr_lo = (wr_pad - wr_hi.astype(F32)).astype(BF16)
    tri = jnp.asarray(np.tril(np.ones((tm, tm), np.float32), -1), BF16)
    if att.ndim == 3:
        b, s, _ = att.shape
        grid = (b, s // tm)
        row = lambda w: pl.BlockSpec((None, tm, w), lambda bi, j: (bi, j, 0))
        mod = pl.BlockSpec((None, 1, d), lambda bi, j: (bi, 0, 0))
        const = lambda shape: pl.BlockSpec(shape, lambda bi, j: (0, 0))
        lead = (b, s)
        tiles = pl.BlockSpec((None, tm * TOK_ROWS, LANES), lambda bi, j: (bi, j, 0))
        tiles_shape = (b, s * TOK_ROWS, LANES)
    else:
        n = att.shape[0]
        grid = (n // tm,)
        row = lambda w: pl.BlockSpec((tm, w), lambda i: (i, 0))
        mod = row(d)
        const = lambda shape: pl.BlockSpec(shape, lambda i: (0, 0))
        lead = (n,)
        tiles = pl.BlockSpec((tm * TOK_ROWS, LANES), lambda i: (i, 0))
        tiles_shape = (n * TOK_ROWS, LANES)
    return pl.pallas_call(
        functools.partial(_merge_kernel, n_axes=len(grid)),
        grid=grid,
        in_specs=[row(ATT_W), row(CONV_W), row(d), mod, mod, mod,
                  const((1, ATT_W)), const((1, CONV_W)), const((1, d)),
                  const((d, d)), const((d, LANES)), const((d, LANES)), const((1, LANES)),
                  const((8, LANES)), const((tm, tm))],
        out_specs=[row(d), tiles, row(LANES), row(LANES), row(LANES), const((8, LANES))],
        out_shape=[jax.ShapeDtypeStruct(lead + (d,), F32),
                   jax.ShapeDtypeStruct(tiles_shape, F32),
                   jax.ShapeDtypeStruct(lead + (LANES,), I32),
                   jax.ShapeDtypeStruct(lead + (LANES,), F32),
                   jax.ShapeDtypeStruct(lead + (LANES,), I32),
                   jax.ShapeDtypeStruct((8, LANES), F32)],
        scratch_shapes=[pltpu.VMEM((8, LANES), F32)],
        compiler_params=_cparams(len(grid)),
        name="merge_route",
    )(att, conv, x, gt1, sh2, sc2, ga, gc, gf, wo_bf, wr_hi, wr_lo, br_pad, cnt0, tri)


def _tok(ref, n, count=1):
    return ref.at[pl.ds(pl.multiple_of(n * TOK_ROWS, TOK_ROWS), count * TOK_ROWS)]


ROW_DMA_UNROLL = 4


def _start_rows(dest_ref, tm, make):
    def start(n, c):
        for t in range(TOP_K):
            make(n, t, dest_ref[0, n * TOP_K + t]).start(priority=t % 2)
        return c
    lax.fori_loop(0, tm, start, 0, unroll=ROW_DMA_UNROLL)


def _wait_rows(dest_ref, tm, make):
    def wait(n, c):
        for t in range(TOP_K):
            make(n, t, dest_ref[0, n * TOP_K + t]).wait()
        return c
    lax.fori_loop(0, tm, wait, 0, unroll=2 * ROW_DMA_UNROLL)


def _row_copies(dest_ref, tm, make):
    _start_rows(dest_ref, tm, make)
    _wait_rows(dest_ref, tm, make)


def _dispatch_kernel(ps_ref, pn_ref, t0_ref, dest_ref, h_ref, dest_s_ref, hs_ref, xs_ref,
                     zbuf, sem, *, tm, n_s, row_tm, n_tiles):
    def make(n, t, d):
        return pltpu.make_async_copy(_tok(h_ref, n), _tok(xs_ref, d), sem)
    _row_copies(dest_ref, tm, make)

    @pl.when(pl.program_id(0) == pl.num_programs(0) - 1)
    def _():
        def make_s(n, t, d):
            return pltpu.make_async_copy(_tok(hs_ref, n), _tok(xs_ref, d), sem)
        _row_copies(dest_s_ref, n_s, make_s)
        _zero_fill(ps_ref, pn_ref, t0_ref, xs_ref, zbuf, sem, row_tm, n_tiles)


def _dispatch(h2, dest, h2_s, dest_s, pad, n_rows, tm=2 * TOK_TM, row_tm=MOE_TM):
    n = h2.shape[0] // TOK_ROWS
    n_s = h2_s.shape[0] // TOK_ROWS
    nt = n // tm
    return pl.pallas_call(
        functools.partial(_dispatch_kernel, tm=tm, n_s=n_s, row_tm=row_tm, n_tiles=n_rows // row_tm),
        grid_spec=pltpu.PrefetchScalarGridSpec(
            num_scalar_prefetch=3,
            grid=(nt,),
            in_specs=[pl.BlockSpec((None, 1, tm * TOP_K), lambda i, *_: (i, 0, 0), memory_space=pltpu.SMEM),
                      pl.BlockSpec((tm * TOK_ROWS, LANES), lambda i, *_: (i, 0)),
                      pl.BlockSpec((1, n_s * TOP_K), lambda i, *_: (0, 0), memory_space=pltpu.SMEM),
                      pl.BlockSpec((n_s * TOK_ROWS, LANES), lambda i, *_: (0, 0))],
            out_specs=pl.BlockSpec(memory_space=pl.ANY),
            scratch_shapes=[pltpu.VMEM((row_tm * TOK_ROWS, LANES), F32), pltpu.SemaphoreType.DMA(())],
        ),
        out_shape=jax.ShapeDtypeStruct((n_rows * TOK_ROWS, LANES), F32),
        compiler_params=_cparams(1),
        name="moe_dispatch",
    )(*pad, dest.reshape(nt, 1, tm * TOP_K), h2, dest_s.reshape(1, n_s * TOP_K), h2_s)


def _zero_fill(ps_ref, pn_ref, t0_ref, xs_ref, zbuf, sem, tm, n_tiles):
    zbuf[...] = jnp.zeros_like(zbuf)
    bits = [tm >> (k + 1) for k in range(tm.bit_length() - 1)]

    def segments(act):
        def seg(e, c):
            off = ps_ref[e]
            ln = pn_ref[e]
            for bit in bits:
                @pl.when((ln & bit) != 0)
                def _(off=off, bit=bit):
                    act(pltpu.make_async_copy(_tok(zbuf, 0, bit), _tok(xs_ref, off, bit), sem))
                off = off + (ln & bit)
            return c
        lax.fori_loop(0, N_EXPERTS, seg, 0)

        def tile(i, c):
            act(pltpu.make_async_copy(zbuf, _tok(xs_ref, i * tm, tm), sem))
            return c
        lax.fori_loop(t0_ref[0], n_tiles, tile, 0)

    segments(lambda cp: cp.start())
    segments(lambda cp: cp.wait())


def _expert_kernel(te_ref, tv_ref, nx_ref, sl_ref, xs_ref, wg_hbm, bg_ref, wu_hbm, bu_ref,
                   wd_hbm, bd_ref, ys_ref, wbuf, wg_bf, wu_bf, wd_bf, sems, *, tm):
    i = pl.program_id(0)
    e = te_ref[i]
    slot = sl_ref[i]
    new_expert = jnp.logical_or(i == 0, e != te_ref[jnp.maximum(i - 1, 0)])

    def fetch(expert, s):
        return [pltpu.make_async_copy(w.at[expert], wbuf.at[s, k], sems.at[s, k])
                for k, w in enumerate((wg_hbm, wu_hbm, wd_hbm))]

    @pl.when(i == 0)
    def _():
        for cp in fetch(e, slot):
            cp.start()

    @pl.when(new_expert)
    def _():
        for cp in fetch(e, slot):
            cp.wait()

        @pl.when(nx_ref[i] >= 0)
        def _():
            for cp in fetch(nx_ref[i], 1 - slot):
                cp.start()
        wg_bf[...] = wbuf[slot, 0].astype(BF16)
        wu_bf[...] = wbuf[slot, 1].astype(BF16)
        wd_bf[...] = wbuf[slot, 2].astype(BF16)

    @pl.when(tv_ref[i] > 0)
    def _():
        x = _load_token_tiles(xs_ref, tm).astype(BF16)
        g = jnp.dot(x, wg_bf[...], preferred_element_type=F32) + bg_ref[...]
        u = jnp.dot(x, wu_bf[...], preferred_element_type=F32) + bu_ref[...]
        g = jnp.minimum(g, SWIGLU_LIMIT)
        u = jnp.clip(u, -SWIGLU_LIMIT, SWIGLU_LIMIT)
        act = (u + 1.0) * g * (1.0 / (1.0 + jnp.exp(-SWIGLU_ALPHA * g)))
        y = jnp.dot(act.astype(BF16), wd_bf[...], preferred_element_type=F32) + bd_ref[...]
        _store_token_tiles(ys_ref, y)

    @pl.when(tv_ref[i] == 0)
    def _():
        ys_ref[...] = jnp.zeros_like(ys_ref)


def _experts(xs, plan, wg, bg, wu, bu, wd, bd, tm=MOE_TM):
    n_tiles = xs.shape[0] // (tm * TOK_ROWS)
    d, f = wg.shape[-2:]
    assert d == f == D_MODEL
    b_spec = lambda n: pl.BlockSpec((None, 1, n), lambda i, te, *_: (te[i], 0, 0))
    rows = pl.BlockSpec((tm * TOK_ROWS, LANES), lambda i, *_: (i, 0))
    hbm = pl.BlockSpec(memory_space=pl.ANY)
    return pl.pallas_call(
        functools.partial(_expert_kernel, tm=tm),
        grid_spec=pltpu.PrefetchScalarGridSpec(
            num_scalar_prefetch=4,
            grid=(n_tiles,),
            in_specs=[rows, hbm, b_spec(f), hbm, b_spec(f), hbm, b_spec(d)],
            out_specs=rows,
            scratch_shapes=[pltpu.VMEM((2, 3, d, f), F32),
                            pltpu.VMEM((d, f), BF16), pltpu.VMEM((d, f), BF16), pltpu.VMEM((f, d), BF16),
                            pltpu.SemaphoreType.DMA((2, 3))],
        ),
        out_shape=jax.ShapeDtypeStruct(xs.shape, F32),
        compiler_params=_cparams(1),
        name="moe_experts",
    )(*plan, xs, wg, bg.reshape(N_EXPERTS, 1, f), wu, bu.reshape(N_EXPERTS, 1, f),
      wd, bd.reshape(N_EXPERTS, 1, d))


def _combine_kernel(dcur_ref, dnxt_ref, ys_ref, x1_ref, gate_ref, gt2_ref, gfin_ref, o_ref,
                    buf, sems, *, tm):
    i = pl.program_id(0)
    cur = i & 1
    slot_toks = TOP_K * tm

    def make(slot):
        def f(n, t, d):
            return pltpu.make_async_copy(_tok(ys_ref, d), _tok(buf, slot * slot_toks + t * tm + n),
                                         sems.at[slot])
        return f

    @pl.when(i == 0)
    def _():
        _start_rows(dcur_ref, tm, make(cur))

    @pl.when(i + 1 < pl.num_programs(0))
    def _():
        _start_rows(dnxt_ref, tm, make(1 - cur))

    _wait_rows(dcur_ref, tm, make(cur))
    base = pl.multiple_of(cur * slot_toks * TOK_ROWS, slot_toks * TOK_ROWS)
    gate = gate_ref[...]
    y = gate[:, 0:1] * _load_token_tiles(buf, tm, base=base)
    for t in range(1, TOP_K):
        y = y + gate[:, t:t + 1] * _load_token_tiles(buf, tm, base=base + t * tm * TOK_ROWS)
    x2 = x1_ref[...] + gt2_ref[...] * y
    o_ref[...] = _rmsnorm(x2, gfin_ref[...])


def _combine(ys, dest, x1, gate, gt2, gfin, seq, tm=TOK_TM):
    d = D_MODEL
    n = x1.shape[0]
    tm = min(tm, n)
    nt = n // tm
    row = lambda w: pl.BlockSpec((tm, w), lambda i: (i, 0))
    if seq:
        mod = pl.BlockSpec((None, 1, d), lambda i: (i // (seq // tm), 0, 0))
    else:
        mod = row(d)
    dspec = lambda fn: pl.BlockSpec((None, 1, tm * TOP_K), lambda i: (fn(i), 0, 0), memory_space=pltpu.SMEM)
    dest3 = dest.reshape(nt, 1, tm * TOP_K)
    return pl.pallas_call(
        functools.partial(_combine_kernel, tm=tm),
        grid=(nt,),
        in_specs=[dspec(lambda i: i), dspec(lambda i: jnp.minimum(i + 1, nt - 1)),
                  pl.BlockSpec(memory_space=pl.ANY), row(d), row(LANES), mod,
                  pl.BlockSpec((1, d), lambda i: (0, 0))],
        out_specs=row(d),
        out_shape=jax.ShapeDtypeStruct(x1.shape, F32),
        scratch_shapes=[pltpu.VMEM((2 * TOP_K * tm * TOK_ROWS, LANES), F32),
                        pltpu.SemaphoreType.DMA((2,))],
        compiler_params=_cparams(1),
        name="moe_combine",
    )(dest3, dest3, ys, x1, gate, gt2, gfin)


def _routing_plan(counts, n_pairs, tm=MOE_TM):
    pc = (counts + tm - 1) // tm * tm
    pend = jnp.cumsum(pc)
    pstart = pend - pc
    n_rows = -(-(n_pairs + N_EXPERTS * (tm - 1)) // tm) * tm
    n_tiles = n_rows // tm
    tile_row = jnp.arange(n_tiles, dtype=I32) * tm
    last_used = jnp.max(jnp.where(pc > 0, jnp.arange(N_EXPERTS, dtype=I32), 0))
    tile_e = jnp.minimum(jnp.sum((tile_row[:, None] >= pend[None, :]).astype(I32), axis=1), last_used)
    tile_valid = (tile_row < pend[-1]).astype(I32)
    ids = jnp.arange(N_EXPERTS, dtype=I32)
    used = pc > 0
    slot_e = (jnp.cumsum(used.astype(I32)) - 1) & 1
    later = jnp.where(used[None, :] & (ids[None, :] > ids[:, None]), ids[None, :], N_EXPERTS)
    next_e = jnp.min(later, axis=1)
    next_e = jnp.where(next_e == N_EXPERTS, -1, next_e)
    pick = (tile_e[:, None] == ids[None, :]).astype(I32)
    plan = (tile_e, tile_valid, jnp.sum(pick * next_e[None, :], axis=1),
            jnp.sum(pick * slot_e[None, :], axis=1))
    pad = ((pstart + counts).astype(I32), (pc - counts).astype(I32),
           (pend[-1:] // tm).astype(I32))
    return pstart.astype(I32), plan, n_rows, pad


def _rope_tables(pos):
    half = HEAD_DIM // 2
    inv = ROPE_THETA ** (-np.arange(half, dtype=np.float64) / half)
    ang = np.asarray(pos, np.float64)[:, None] * inv[None, :]
    cos = np.cos(ang)
    sin = np.sin(ang)
    cosf = np.concatenate([cos, cos, cos, cos], axis=1).astype(np.float32)
    sinf = np.concatenate([-sin, sin, -sin, sin], axis=1).astype(np.float32)
    return jnp.asarray(cosf), jnp.asarray(sinf)


def kernel(x_prompt, x_sample, cache_k, cache_v, state_conv, c_prompt, c_sample, w_ada, b_ada,
           g_norm_mix, w_in, conv_w, g_attn_out, g_conv_out, w_out, g_norm_ffn, w_router, b_router,
           w_gate, b_gate, w_up, b_up, w_down, b_down, g_final):
    depth = w_in.shape[0]
    assert depth == 1, "single-layer trunk"
    bp, sp, d = x_prompt.shape
    bs, ts, _ = x_sample.shape
    ns = bs * ts
    n_cache = cache_k.shape[2]
    l = 0

    mods = _modulations(jnp.concatenate([c_prompt, c_sample], axis=0), w_ada[l], b_ada[l])
    mp = [m.reshape(bp, 1, d) for m in jnp.split(mods[:bp], 6, axis=-1)]
    ms = [jnp.repeat(m, ts, axis=0) for m in jnp.split(mods[bp:], 6, axis=-1)]

    w_in_bf = w_in[l].astype(BF16)
    w_out_bf = w_out[l].astype(BF16)
    wr_pad = jnp.pad(w_router[l], ((0, 0), (0, LANES - N_EXPERTS)))
    br_pad = jnp.pad(b_router[l].reshape(1, N_EXPERTS), ((0, 0), (0, LANES - N_EXPERTS)),
                     constant_values=NEG)
    g_mix = g_norm_mix[l].reshape(1, d)
    g_ffn = g_norm_ffn[l].reshape(1, d)
    g_att = g_attn_out[l].reshape(1, ATT_W)
    g_cnv = g_conv_out[l].reshape(1, CONV_W)
    g_fin = g_final.reshape(1, d)

    cos_p, sin_p = _rope_tables(np.arange(sp))
    q_p, k_p, v_p, conv_p, tail_p = _inproj_prompt(x_prompt, mp[0], mp[1], g_mix, w_in_bf,
                                                   cos_p, sin_p, conv_w[l])
    att_p = _attn_prompt(q_p, k_p, v_p)

    xs_rows = x_sample.reshape(ns, d)
    cos_s, sin_s = _rope_tables(np.tile(PAST_LEN + np.arange(ts), bs))
    st = state_conv[l]
    zrow = jnp.zeros((bs, 1, CONV_W), F32)
    s1 = jnp.concatenate([st[:, 1:2], zrow, zrow, zrow], axis=1).reshape(ns, CONV_W)
    s2 = jnp.concatenate([st[:, 0:1], st[:, 1:2], zrow, zrow], axis=1).reshape(ns, CONV_W)
    q_s, k_s, v_s, conv_s, cu_s = _inproj_sample(xs_rows, ms[0], ms[1], g_mix, w_in_bf,
                                                 cos_s, sin_s, conv_w[l], s1, s2, ts)
    rows8 = lambda a: jnp.pad(a.reshape(bs, ts, ATT_W), ((0, 0), (0, 8 - ts), (0, 0)))
    kt = jnp.transpose(cache_k[l], (0, 2, 3, 1))
    vt = jnp.transpose(cache_v[l], (0, 2, 3, 1))
    att_s = _attn_sample(rows8(q_s), kt, vt, rows8(k_s), rows8(v_s), ts)[:, :ts].reshape(ns, ATT_W)

    cnt0 = jnp.zeros((8, LANES), F32)
    x1_p, h2_p, idx_p, gate_p, rank_p, cnt_p = _merge(
        att_p, conv_p, x_prompt, mp[2], mp[3], mp[4], g_att, g_cnv, g_ffn,
        w_out_bf, wr_pad, br_pad, cnt0, 2 * TOK_TM)
    x1_s, h2_s, idx_s, gate_s, rank_s, cnt_all = _merge(
        att_s, conv_s, xs_rows, ms[2], ms[3], ms[4], g_att, g_cnv, g_ffn,
        w_out_bf, wr_pad, br_pad, cnt_p, ns)

    n_tok = bp * sp + ns
    counts = cnt_all[0, :N_EXPERTS].astype(I32)
    pstart, plan, n_rows, pad = _routing_plan(counts, n_tok * TOP_K)
    ids = jnp.arange(N_EXPERTS, dtype=I32)
    slot_of = lambda idx, rank: jnp.sum(
        jnp.where(idx[..., :TOP_K, None] == ids, pstart, 0), axis=-1) + rank[..., :TOP_K]
    dest_p = slot_of(idx_p, rank_p).reshape(bp * sp, TOP_K)
    dest_s = slot_of(idx_s, rank_s)
    xs_sorted = _dispatch(h2_p.reshape(bp * sp * TOK_ROWS, LANES), dest_p, h2_s, dest_s, pad, n_rows)
    ys = _experts(xs_sorted, plan, w_gate[l], b_gate[l], w_up[l], b_up[l], w_down[l], b_down[l])
    y_prompt = _combine(ys, dest_p, x1_p.reshape(bp * sp, d), gate_p.reshape(bp * sp, LANES),
                        mp[5], g_fin, sp).reshape(bp, sp, d)
    y_sample = _combine(ys, dest_s, x1_s, gate_s, ms[5], g_fin, 0).reshape(bs, ts, d)

    heads = lambda a, b, s: a.reshape(1, b, s, N_HEADS, HEAD_DIM)
    keep = min(WINDOW_MAX, sp)
    return (y_prompt, y_sample,
            heads(k_p, bp, sp)[:, :, sp - keep:], heads(v_p, bp, sp)[:, :, sp - keep:],
            tail_p[:, 8 - (CONV_K - 1):][None],
            heads(k_s, bs, ts), heads(v_s, bs, ts),
            cu_s.reshape(bs, ts, CONV_W)[:, ts - (CONV_K - 1):][None])
```

```python
import functools

import jax
import jax.numpy as jnp
import numpy as np
from jax import lax
from jax.experimental import pallas as pl
from jax.experimental.pallas import tpu as pltpu

F32 = jnp.float32
BF16 = jnp.bfloat16
I32 = jnp.int32
HIGHEST = lax.Precision.HIGHEST

D_MODEL = 1024
HEAD_DIM = 64
N_HEADS = 12
ATT_W = N_HEADS * HEAD_DIM
CONV_W = D_MODEL - ATT_W
CONV_K = 3
PATTERNS = ((128, 1), (512, 4), (2048, 16))
WINDOW_MAX = 2048
PAST_LEN = 16384
ROPE_THETA = 10000.0
N_EXPERTS = 32
TOP_K = 4
SWIGLU_ALPHA = 1.702
SWIGLU_LIMIT = 7.0
NORM_EPS = 1e-6
NEG = -1e30
IN_W = 3 * ATT_W + 3 * CONV_W

LANES = 128
Q_BLK = 128
NEAR_W = 512
FAR_D = 16
MOE_TM = 256
TOK_TM = 256
VMEM_LIMIT = 56 * 1024 * 1024
TOK_ROWS = D_MODEL // LANES


def _cparams(n_axes, vmem=VMEM_LIMIT):
    return pltpu.CompilerParams(dimension_semantics=("arbitrary",) * n_axes,
                                vmem_limit_bytes=vmem)


def _multiplicity(delta):
    delta = np.asarray(delta)
    c = np.zeros(delta.shape, np.float32)
    for w, d in PATTERNS:
        c += ((delta >= 0) & (delta <= w) & (delta % d == 0)).astype(np.float32)
    return c


def _ada_kernel(c_ref, w_ref, b_ref, o_ref):
    c = c_ref[...]
    s = c / (1.0 + jnp.exp(-c))
    o_ref[...] = jnp.dot(s, w_ref[...], precision=HIGHEST,
                         preferred_element_type=F32) + b_ref[...]


def _modulations(c_all, w_ada, b_ada):
    r, d = c_all.shape
    n = w_ada.shape[1]
    tn = 1536
    return pl.pallas_call(
        _ada_kernel,
        grid=(n // tn,),
        in_specs=[pl.BlockSpec((r, d), lambda j: (0, 0)),
                  pl.BlockSpec((d, tn), lambda j: (0, j)),
                  pl.BlockSpec((1, tn), lambda j: (0, j))],
        out_specs=pl.BlockSpec((r, tn), lambda j: (0, j)),
        out_shape=jax.ShapeDtypeStruct((r, n), F32),
        compiler_params=_cparams(1),
        name="ada_modulation",
    )(c_all, w_ada, b_ada.reshape(1, n))


def _norm_mod(x, g, shift, scale):
    ms = jnp.mean(x * x, axis=-1, keepdims=True)
    return (x * lax.rsqrt(ms + NORM_EPS) * g) * (1.0 + scale) + shift


def _rmsnorm(x, g):
    ms = jnp.mean(x * x, axis=-1, keepdims=True)
    return x * lax.rsqrt(ms + NORM_EPS) * g


def _store_token_tiles(ref, x, base=0):
    tm = x.shape[0]
    for c in range(TOK_ROWS):
        ref[pl.ds(base + c, tm, stride=TOK_ROWS), :] = x[:, c * LANES:(c + 1) * LANES]


def _load_token_tiles(ref, tm, base=0):
    return jnp.concatenate([ref[pl.ds(base + c, tm, stride=TOK_ROWS), :] for c in range(TOK_ROWS)],
                           axis=1)


def _swap_halves(xc):
    lane = lax.broadcasted_iota(I32, xc.shape, 1)
    first = (lane & (HEAD_DIM - 1)) < HEAD_DIM // 2
    return jnp.where(first, pltpu.roll(xc, LANES - HEAD_DIM // 2, 1),
                     pltpu.roll(xc, HEAD_DIM // 2, 1))


def _rope(x, cosf, sinf):
    outs = []
    for c in range(x.shape[1] // LANES):
        xc = x[:, c * LANES:(c + 1) * LANES]
        outs.append(xc * cosf + _swap_halves(xc) * sinf)
    return jnp.concatenate(outs, axis=1)


def _inproj_kernel(*refs, tm, sample, seq_per_batch):
    if sample:
        (x_ref, sh_ref, sc_ref, g_ref, w_ref, cos_ref, sin_ref, cw_ref, s1_ref, s2_ref,
         q_ref, k_ref, v_ref, conv_ref, cu_ref, cu_ext) = refs
    else:
        (x_ref, sh_ref, sc_ref, g_ref, w_ref, cos_ref, sin_ref, cw_ref,
         q_ref, k_ref, v_ref, conv_ref, tail_ref, cu_ext) = refs
    h = _norm_mod(x_ref[...], g_ref[...], sh_ref[...], sc_ref[...])
    z = jnp.dot(h.astype(BF16), w_ref[...], preferred_element_type=F32)
    cosf = cos_ref[...]
    sinf = sin_ref[...]
    q_ref[...] = _rope(z[:, 0:ATT_W], cosf, sinf) * (HEAD_DIM ** -0.5)
    k_ref[...] = _rope(z[:, ATT_W:2 * ATT_W], cosf, sinf)
    v_ref[...] = z[:, 2 * ATT_W:3 * ATT_W]
    o = 3 * ATT_W
    gb = z[:, o:o + CONV_W]
    cu = z[:, o + CONV_W:o + 2 * CONV_W] * z[:, o + 2 * CONV_W:o + 3 * CONV_W]
    if sample:
        cu_ext[0:8, :] = jnp.zeros((8, CONV_W), F32)
    else:
        @pl.when(pl.program_id(1) == 0)
        def _():
            cu_ext[0:8, :] = jnp.zeros((8, CONV_W), F32)
    cu_ext[8:8 + tm, :] = cu
    p1 = cu_ext[7:7 + tm, :]
    p2 = cu_ext[6:6 + tm, :]
    if sample:
        t = lax.broadcasted_iota(I32, (tm, CONV_W), 0) % seq_per_batch
        p1 = jnp.where(t >= 1, p1, 0.0) + s1_ref[...]
        p2 = jnp.where(t >= 2, p2, 0.0) + s2_ref[...]
        cu_ref[...] = cu
    cw = cw_ref[...]
    conv_ref[...] = gb * (cw[0:1, :] * p2 + cw[1:2, :] * p1 + cw[2:3, :] * cu)
    if not sample:
        tail = cu_ext[tm:tm + 8, :]
        tail_ref[...] = tail
        cu_ext[0:8, :] = tail


def _inproj_prompt(x, shift, scale, g, w_bf, cosf, sinf, conv_w, tm=512):
    b, s, d = x.shape
    row = lambda bi, j: (bi, j, 0)
    per_b = lambda bi, j: (bi, 0, 0)
    const = lambda bi, j: (0, 0)
    outs = pl.pallas_call(
        functools.partial(_inproj_kernel, tm=tm, sample=False, seq_per_batch=s),
        grid=(b, s // tm),
        in_specs=[pl.BlockSpec((None, tm, d), row),
                  pl.BlockSpec((None, 1, d), per_b),
                  pl.BlockSpec((None, 1, d), per_b),
                  pl.BlockSpec((1, d), const),
                  pl.BlockSpec((d, IN_W), const),
                  pl.BlockSpec((tm, LANES), lambda bi, j: (j, 0)),
                  pl.BlockSpec((tm, LANES), lambda bi, j: (j, 0)),
                  pl.BlockSpec((CONV_K, CONV_W), const)],
        out_specs=[pl.BlockSpec((None, tm, ATT_W), row),
                   pl.BlockSpec((None, tm, ATT_W), row),
                   pl.BlockSpec((None, tm, ATT_W), row),
                   pl.BlockSpec((None, tm, CONV_W), row),
                   pl.BlockSpec((None, 8, CONV_W), per_b)],
        out_shape=[jax.ShapeDtypeStruct((b, s, ATT_W), F32),
                   jax.ShapeDtypeStruct((b, s, ATT_W), F32),
                   jax.ShapeDtypeStruct((b, s, ATT_W), F32),
                   jax.ShapeDtypeStruct((b, s, CONV_W), F32),
                   jax.ShapeDtypeStruct((b, 8, CONV_W), F32)],
        scratch_shapes=[pltpu.VMEM((tm + 8, CONV_W), F32)],
        compiler_params=_cparams(2),
        name="inproj_prompt",
    )(x, shift, scale, g, w_bf, cosf, sinf, conv_w)
    return outs


def _inproj_sample(x, shift, scale, g, w_bf, cosf, sinf, conv_w, s1, s2, seq_per_batch):
    n, d = x.shape
    full = lambda shape: pl.BlockSpec(shape, lambda i: (0, 0))
    outs = pl.pallas_call(
        functools.partial(_inproj_kernel, tm=n, sample=True, seq_per_batch=seq_per_batch),
        grid=(1,),
        in_specs=[full((n, d)), full((n, d)), full((n, d)), full((1, d)), full((d, IN_W)),
                  full((n, LANES)), full((n, LANES)), full((CONV_K, CONV_W)),
                  full((n, CONV_W)), full((n, CONV_W))],
        out_specs=[full((n, ATT_W)), full((n, ATT_W)), full((n, ATT_W)),
                   full((n, CONV_W)), full((n, CONV_W))],
        out_shape=[jax.ShapeDtypeStruct((n, ATT_W), F32)] * 3
        + [jax.ShapeDtypeStruct((n, CONV_W), F32)] * 2,
        scratch_shapes=[pltpu.VMEM((n + 8, CONV_W), F32)],
        compiler_params=_cparams(1),
        name="inproj_sample",
    )(x, shift, scale, g, w_bf, cosf, sinf, conv_w, s1, s2)
    return outs


def _stack_heads(x):
    lo = lax.broadcasted_iota(I32, x.shape, 1) < HEAD_DIM
    zero = jnp.zeros_like(x)
    return jnp.concatenate([jnp.where(lo, x, zero), jnp.where(lo, zero, x)], axis=0)


def _split_pv(r0, r1, m, rows):
    lo = lax.broadcasted_iota(I32, r0.shape, 1) < HEAD_DIM
    num = jnp.where(lo, r0, r1)
    den = jnp.where(lo, pltpu.roll(r0, HEAD_DIM, 1), pltpu.roll(r1, HEAD_DIM, 1))
    mx = jnp.where(lo, jnp.broadcast_to(m[:rows], r0.shape), jnp.broadcast_to(m[rows:], r0.shape))
    return num, den, mx


def _attn_prompt_kernel(q_ref, k_ref, v_ref, lmult_ref, o_ref,
                        kpad, v0pad, v1pad, num3, den3, max3,
                        *, seq, far_unroll, near_unroll, nq, far_s):
    nt = (((1,), (1,)), ((), ()))
    n_blk = seq // Q_BLK
    lo1 = lax.broadcasted_iota(I32, (Q_BLK, LANES), 1) < HEAD_DIM
    ones = jnp.ones((Q_BLK, LANES), F32)

    zpad = jnp.zeros((NEAR_W, LANES), BF16)
    kpad[0:NEAR_W, :] = zpad
    v0pad[0:NEAR_W, :] = zpad
    v1pad[0:NEAR_W, :] = zpad

    def fill(i, c):
        s0 = pl.multiple_of(i * Q_BLK, Q_BLK)
        d0 = pl.multiple_of(i * Q_BLK + NEAR_W, Q_BLK)
        kpad[pl.ds(d0, Q_BLK), :] = k_ref[pl.ds(s0, Q_BLK), :].astype(BF16)
        vb = v_ref[pl.ds(s0, Q_BLK), :]
        v0pad[pl.ds(d0, Q_BLK), :] = jnp.where(lo1, vb, ones).astype(BF16)
        v1pad[pl.ds(d0, Q_BLK), :] = jnp.where(lo1, ones, vb).astype(BF16)
        return c
    lax.fori_loop(0, n_blk, fill, 0)

    nf = seq // far_s
    row = lax.broadcasted_iota(I32, (2 * nf, nf), 0) & (nf - 1)
    col = lax.broadcasted_iota(I32, (2 * nf, nf), 1)
    allowed = jnp.logical_and(col <= row, ((row - col) & (FAR_D // far_s - 1)) == 0)
    lof = lax.broadcasted_iota(I32, (nf, LANES), 1) < HEAD_DIM
    onef = jnp.ones((nf, LANES), F32)

    def far(r, c):
        sl = pl.ds(r, nf, stride=far_s)
        q2 = _stack_heads(q_ref[sl, :]).astype(BF16)
        kr = k_ref[sl, :].astype(BF16)
        vr = v_ref[sl, :]
        s = lax.dot_general(q2, kr, nt, preferred_element_type=F32)
        s = jnp.where(allowed, s, NEG)
        m = jnp.max(s, axis=1, keepdims=True)
        p = jnp.exp(s - m).astype(BF16)
        r0 = jnp.dot(p[:nf], jnp.where(lof, vr, onef).astype(BF16), preferred_element_type=F32)
        r1 = jnp.dot(p[nf:], jnp.where(lof, onef, vr).astype(BF16), preferred_element_type=F32)
        num, den, mx = _split_pv(r0, r1, m, nf)
        num3[sl, :] = num
        den3[sl, :] = den
        max3[sl, :] = mx
        return c
    lax.fori_loop(0, far_s, far, 0, unroll=far_unroll)

    nk = NEAR_W + nq
    kcol = lax.broadcasted_iota(I32, (1, nk), 1)

    def near(first_blocks):
        def body(i, c):
            s0 = pl.multiple_of(i * nq, nq)
            q2 = _stack_heads(q_ref[pl.ds(s0, nq), :]).astype(BF16)
            kw = kpad[pl.ds(s0, nk), :]
            s = lax.dot_general(q2, kw, nt, preferred_element_type=F32) + lmult_ref[...]
            if first_blocks:
                s = jnp.where(kcol >= NEAR_W - s0, s, NEG)
            m = jnp.max(s, axis=1, keepdims=True)
            p = jnp.exp(s - m).astype(BF16)
            r0 = jnp.dot(p[:nq], v0pad[pl.ds(s0, nk), :], preferred_element_type=F32)
            r1 = jnp.dot(p[nq:], v1pad[pl.ds(s0, nk), :], preferred_element_type=F32)
            num, den, mx = _split_pv(r0, r1, m, nq)
            mx3 = max3[pl.ds(s0, nq), :]
            mm = jnp.maximum(mx, mx3)
            a = jnp.exp(mx - mm)
            b = jnp.exp(mx3 - mm)
            o_ref[pl.ds(s0, nq), :] = ((num * a + num3[pl.ds(s0, nq), :] * b)
                                       / (den * a + den3[pl.ds(s0, nq), :] * b))
            return c
        return body

    n_first = NEAR_W // nq
    lax.fori_loop(0, n_first, near(True), 0, unroll=near_unroll)
    lax.fori_loop(n_first, seq // nq, near(False), 0, unroll=near_unroll)


def _near_table(nq):
    i = np.arange(nq)[:, None]
    kl = np.arange(NEAR_W + nq)[None, :]
    delta = i + NEAR_W - kl
    mult = np.zeros(delta.shape, np.float32)
    for w, d in PATTERNS[:2]:
        mult += ((delta >= 0) & (delta <= w) & (delta % d == 0)).astype(np.float32)
    lmult = np.where(mult > 0, np.log(np.maximum(mult, 1.0)), NEG).astype(np.float32)
    return np.tile(lmult, (2, 1))


def _attn_prompt(q, k, v, far_unroll=4, near_unroll=3, nq=2 * Q_BLK, far_s=FAR_D):
    b, s, _ = q.shape
    lmult = _near_table(nq)
    blk = pl.BlockSpec((None, s, LANES), lambda bi, hp: (bi, 0, hp))
    tab = pl.BlockSpec((2 * nq, NEAR_W + nq), lambda bi, hp: (0, 0))
    return pl.pallas_call(
        functools.partial(_attn_prompt_kernel, seq=s, far_unroll=far_unroll, near_unroll=near_unroll,
                          nq=nq, far_s=far_s),
        grid=(b, ATT_W // LANES),
        in_specs=[blk, blk, blk, tab],
        out_specs=blk,
        out_shape=jax.ShapeDtypeStruct((b, s, ATT_W), F32),
        scratch_shapes=[pltpu.VMEM((s + NEAR_W, LANES), BF16)] * 3
        + [pltpu.VMEM((s, LANES), F32)] * 3,
        compiler_params=_cparams(2),
        name="attn_prompt",
    )(q, k, v, jnp.asarray(lmult))


def _attn_sample_kernel(q_ref, kt_ref, vt_ref, kn_ref, vn_ref, bias_ref, mult_ref, o_ref):
    nt = (((1,), (1,)), ((), ()))
    q = q_ref[...]
    bias = bias_ref[...]
    mult = mult_ref[...]
    n_c = kt_ref.shape[-1]
    kn = kn_ref[...]
    vn = vn_ref[...]
    zrows = jnp.zeros((LANES - 8, HEAD_DIM), F32)
    for h in range(N_HEADS):
        hs = slice(h * HEAD_DIM, (h + 1) * HEAD_DIM)
        qh = q[:, hs].astype(BF16)
        knh = jnp.concatenate([kn[:, hs], zrows], axis=0).astype(BF16)
        vnh = jnp.concatenate([vn[:, hs], zrows], axis=0).astype(BF16)
        s = jnp.concatenate(
            [jnp.dot(qh, kt_ref[h].astype(BF16), preferred_element_type=F32),
             lax.dot_general(qh, knh, nt, preferred_element_type=F32)], axis=1) + bias
        m = jnp.max(s, axis=1, keepdims=True)
        p = jnp.exp(s - m) * mult
        den = jnp.sum(p, axis=1, keepdims=True)
        pb = p.astype(BF16)
        num = (lax.dot_general(pb[:, :n_c], vt_ref[h].astype(BF16), nt, preferred_element_type=F32)
               + jnp.dot(pb[:, n_c:], vnh, preferred_element_type=F32))
        o_ref[:, hs] = num / den


def _sample_tables(n_cache, t_new):
    t = np.arange(8)[:, None] % t_new
    rho = np.arange(n_cache)[None, :]
    c_cache = _multiplicity(t + n_cache - rho)
    tp = np.arange(LANES)[None, :]
    c_new = np.where(tp < t_new, _multiplicity(t - tp), 0.0)
    mult = np.concatenate([c_cache, c_new], axis=1).astype(np.float32)
    bias = np.where(mult > 0, 0.0, NEG).astype(np.float32)
    return bias, mult


def _attn_sample(q8, kt, vt, kn8, vn8, t_new):
    b = q8.shape[0]
    n_c = kt.shape[-1]
    bias, mult = _sample_tables(n_c, t_new)
    cache = pl.BlockSpec((None, N_HEADS, HEAD_DIM, n_c), lambda i: (i, 0, 0, 0))
    row = pl.BlockSpec((None, 8, ATT_W), lambda i: (i, 0, 0))
    tab = pl.BlockSpec((8, n_c + LANES), lambda i: (0, 0))
    return pl.pallas_call(
        _attn_sample_kernel,
        grid=(b,),
        in_specs=[row, cache, cache, row, row, tab, tab],
        out_specs=row,
        out_shape=jax.ShapeDtypeStruct((b, 8, ATT_W), F32),
        compiler_params=_cparams(1),
        name="attn_sample",
    )(q8, kt, vt, kn8, vn8, jnp.asarray(bias), jnp.asarray(mult))


def _merge_kernel(att_ref, conv_ref, x_ref, gt1_ref, sh2_ref, sc2_ref, ga_ref, gc_ref, gf_ref,
                  wo_ref, wr_ref, wrl_ref, br_ref, cnt0_ref, tri_ref,
                  x1_ref, h2_ref, idx_ref, gate_ref, rank_ref, cnt_ref, cnt_sc, *, n_axes):
    first = pl.program_id(0) == 0
    if n_axes == 2:
        first = jnp.logical_and(first, pl.program_id(1) == 0)

    @pl.when(first)
    def _():
        cnt_sc[...] = cnt0_ref[...]

    an = _rmsnorm(att_ref[...], ga_ref[...]).astype(BF16)
    cn = _rmsnorm(conv_ref[...], gc_ref[...]).astype(BF16)
    mix = (jnp.dot(an, wo_ref[0:ATT_W, :], preferred_element_type=F32)
           + jnp.dot(cn, wo_ref[ATT_W:D_MODEL, :], preferred_element_type=F32))
    x1 = x_ref[...] + gt1_ref[...] * mix
    x1_ref[...] = x1
    h2 = _norm_mod(x1, gf_ref[...], sh2_ref[...], sc2_ref[...])
    _store_token_tiles(h2_ref, h2)
    hi = h2.astype(BF16)
    lo = (h2 - hi.astype(F32)).astype(BF16)
    logits = (jnp.dot(hi, wr_ref[...], preferred_element_type=F32)
              + (jnp.dot(hi, wrl_ref[...], preferred_element_type=F32)
                 + jnp.dot(lo, wr_ref[...], preferred_element_type=F32))) + br_ref[...]
    tm = logits.shape[0]
    lane = lax.broadcasted_iota(I32, (tm, LANES), 1)
    work = logits
    vals, idxs = [], []
    for _ in range(TOP_K):
        mx = jnp.max(work, axis=1, keepdims=True)
        ix = jnp.min(jnp.where(work == mx, lane, LANES), axis=1, keepdims=True)
        vals.append(mx)
        idxs.append(ix)
        work = jnp.where(lane == ix, 3.0 * NEG, work)
    es = [jnp.exp(v - vals[0]) for v in vals]
    den = es[0] + es[1] + es[2] + es[3]
    onehot = jnp.zeros((tm, LANES), F32)
    for ix in idxs:
        onehot = onehot + (lane == ix).astype(F32)
    before = jnp.dot(tri_ref[...], onehot.astype(BF16), preferred_element_type=F32) + cnt_sc[0:1, :]
    idx_o = jnp.zeros((tm, LANES), I32)
    gate_o = jnp.zeros((tm, LANES), F32)
    rank_o = jnp.zeros((tm, LANES), F32)
    for t in range(TOP_K):
        rk = jnp.sum(jnp.where(lane == idxs[t], before, 0.0), axis=1, keepdims=True)
        idx_o = jnp.where(lane == t, idxs[t], idx_o)
        gate_o = jnp.where(lane == t, es[t] / den, gate_o)
        rank_o = jnp.where(lane == t, rk, rank_o)
    idx_ref[...] = idx_o
    gate_ref[...] = gate_o
    rank_ref[...] = rank_o.astype(I32)
    cnt_sc[...] = cnt_sc[...] + jnp.sum(onehot, axis=0, keepdims=True)
    cnt_ref[...] = cnt_sc[...]


def _merge(att, conv, x, gt1, sh2, sc2, ga, gc, gf, wo_bf, wr_pad, br_pad, cnt0, tm):
    d = D_MODEL
    wr_hi = wr_pad.astype(BF16)
    wr_lo = (wr_pad - wr_hi.astype(F32)).astype(BF16)
    tri = jnp.asarray(np.tril(np.ones((tm, tm), np.float32), -1), BF16)
    if att.ndim == 3:
        b, s, _ = att.shape
        grid = (b, s // tm)
        row = lambda w: pl.BlockSpec((None, tm, w), lambda bi, j: (bi, j, 0))
        mod = pl.BlockSpec((None, 1, d), lambda bi, j: (bi, 0, 0))
        const = lambda shape: pl.BlockSpec(shape, lambda bi, j: (0, 0))
        lead = (b, s)
        tiles = pl.BlockSpec((None, tm * TOK_ROWS, LANES), lambda bi, j: (bi, j, 0))
        tiles_shape = (b, s * TOK_ROWS, LANES)
    else:
        n = att.shape[0]
        grid = (n // tm,)
        row = lambda w: pl.BlockSpec((tm, w), lambda i: (i, 0))
        mod = row(d)
        const = lambda shape: pl.BlockSpec(shape, lambda i: (0, 0))
        lead = (n,)
        tiles = pl.BlockSpec((tm * TOK_ROWS, LANES), lambda i: (i, 0))
        tiles_shape = (n * TOK_ROWS, LANES)
    return pl.pallas_call(
        functools.partial(_merge_kernel, n_axes=len(grid)),
        grid=grid,
        in_specs=[row(ATT_W), row(CONV_W), row(d), mod, mod, mod,
                  const((1, ATT_W)), const((1, CONV_W)), const((1, d)),
                  const((d, d)), const((d, LANES)), const((d, LANES)), const((1, LANES)),
                  const((8, LANES)), const((tm, tm))],
        out_specs=[row(d), tiles, row(LANES), row(LANES), row(LANES), const((8, LANES))],
        out_shape=[jax.ShapeDtypeStruct(lead + (d,), F32),
                   jax.ShapeDtypeStruct(tiles_shape, F32),
                   jax.ShapeDtypeStruct(lead + (LANES,), I32),
                   jax.ShapeDtypeStruct(lead + (LANES,), F32),
                   jax.ShapeDtypeStruct(lead + (LANES,), I32),
                   jax.ShapeDtypeStruct((8, LANES), F32)],
        scratch_shapes=[pltpu.VMEM((8, LANES), F32)],
        compiler_params=_cparams(len(grid)),
        name="merge_route",
    )(att, conv, x, gt1, sh2, sc2, ga, gc, gf, wo_bf, wr_hi, wr_lo, br_pad, cnt0, tri)


def _tok(ref, n, count=1):
    return ref.at[pl.ds(pl.multiple_of(n * TOK_ROWS, TOK_ROWS), count * TOK_ROWS)]


ROW_DMA_UNROLL = 4


def _start_rows(dest_ref, tm, make):
    def start(n, c):
        for t in range(TOP_K):
            make(n, t, dest_ref[0, n * TOP_K + t]).start(priority=t % 2)
        return c
    lax.fori_loop(0, tm, start, 0, unroll=ROW_DMA_UNROLL)


def _wait_rows(dest_ref, tm, make):
    def wait(n, c):
        for t in range(TOP_K):
            make(n, t, dest_ref[0, n * TOP_K + t]).wait()
        return c
    lax.fori_loop(0, tm, wait, 0, unroll=2 * ROW_DMA_UNROLL)


def _row_copies(dest_ref, tm, make):
    _start_rows(dest_ref, tm, make)
    _wait_rows(dest_ref, tm, make)


def _dispatch_kernel(ps_ref, pn_ref, t0_ref, dest_ref, h_ref, dest_s_ref, hs_ref, xs_ref,
                     zbuf, sem, *, tm, n_s, row_tm, n_tiles):
    def make(n, t, d):
        return pltpu.make_async_copy(_tok(h_ref, n), _tok(xs_ref, d), sem)
    _row_copies(dest_ref, tm, make)

    @pl.when(pl.program_id(0) == pl.num_programs(0) - 1)
    def _():
        def make_s(n, t, d):
            return pltpu.make_async_copy(_tok(hs_ref, n), _tok(xs_ref, d), sem)
        _row_copies(dest_s_ref, n_s, make_s)
        _zero_fill(ps_ref, pn_ref, t0_ref, xs_ref, zbuf, sem, row_tm, n_tiles)


def _dispatch(h2, dest, h2_s, dest_s, pad, n_rows, tm=2 * TOK_TM, row_tm=MOE_TM):
    n = h2.shape[0] // TOK_ROWS
    n_s = h2_s.shape[0] // TOK_ROWS
    nt = n // tm
    return pl.pallas_call(
        functools.partial(_dispatch_kernel, tm=tm, n_s=n_s, row_tm=row_tm, n_tiles=n_rows // row_tm),
        grid_spec=pltpu.PrefetchScalarGridSpec(
            num_scalar_prefetch=3,
            grid=(nt,),
            in_specs=[pl.BlockSpec((None, 1, tm * TOP_K), lambda i, *_: (i, 0, 0), memory_space=pltpu.SMEM),
                      pl.BlockSpec((tm * TOK_ROWS, LANES), lambda i, *_: (i, 0)),
                      pl.BlockSpec((1, n_s * TOP_K), lambda i, *_: (0, 0), memory_space=pltpu.SMEM),
                      pl.BlockSpec((n_s * TOK_ROWS, LANES), lambda i, *_: (0, 0))],
            out_specs=pl.BlockSpec(memory_space=pl.ANY),
            scratch_shapes=[pltpu.VMEM((row_tm * TOK_ROWS, LANES), F32), pltpu.SemaphoreType.DMA(())],
        ),
        out_shape=jax.ShapeDtypeStruct((n_rows * TOK_ROWS, LANES), F32),
        compiler_params=_cparams(1),
        name="moe_dispatch",
    )(*pad, dest.reshape(nt, 1, tm * TOP_K), h2, dest_s.reshape(1, n_s * TOP_K), h2_s)


def _zero_fill(ps_ref, pn_ref, t0_ref, xs_ref, zbuf, sem, tm, n_tiles):
    zbuf[...] = jnp.zeros_like(zbuf)
    bits = [tm >> (k + 1) for k in range(tm.bit_length() - 1)]

    def segments(act):
        def seg(e, c):
            off = ps_ref[e]
            ln = pn_ref[e]
            for bit in bits:
                @pl.when((ln & bit) != 0)
                def _(off=off, bit=bit):
                    act(pltpu.make_async_copy(_tok(zbuf, 0, bit), _tok(xs_ref, off, bit), sem))
                off = off + (ln & bit)
            return c
        lax.fori_loop(0, N_EXPERTS, seg, 0)

        def tile(i, c):
            act(pltpu.make_async_copy(zbuf, _tok(xs_ref, i * tm, tm), sem))
            return c
        lax.fori_loop(t0_ref[0], n_tiles, tile, 0)

    segments(lambda cp: cp.start())
    segments(lambda cp: cp.wait())


def _expert_kernel(te_ref, tv_ref, nx_ref, sl_ref, xs_ref, wg_hbm, bg_ref, wu_hbm, bu_ref,
                   wd_hbm, bd_ref, ys_ref, wbuf, wg_bf, wu_bf, wd_bf, sems, *, tm):
    i = pl.program_id(0)
    e = te_ref[i]
    slot = sl_ref[i]
    new_expert = jnp.logical_or(i == 0, e != te_ref[jnp.maximum(i - 1, 0)])

    def fetch(expert, s):
        return [pltpu.make_async_copy(w.at[expert], wbuf.at[s, k], sems.at[s, k])
                for k, w in enumerate((wg_hbm, wu_hbm, wd_hbm))]

    @pl.when(i == 0)
    def _():
        for cp in fetch(e, slot):
            cp.start()

    @pl.when(new_expert)
    def _():
        for cp in fetch(e, slot):
            cp.wait()

        @pl.when(nx_ref[i] >= 0)
        def _():
            for cp in fetch(nx_ref[i], 1 - slot):
                cp.start(priority=1)
        wg_bf[...] = wbuf[slot, 0].astype(BF16)
        wu_bf[...] = wbuf[slot, 1].astype(BF16)
        wd_bf[...] = wbuf[slot, 2].astype(BF16)

    @pl.when(tv_ref[i] > 0)
    def _():
        x = _load_token_tiles(xs_ref, tm).astype(BF16)
        g = jnp.dot(x, wg_bf[...], preferred_element_type=F32) + bg_ref[...]
        u = jnp.dot(x, wu_bf[...], preferred_element_type=F32) + bu_ref[...]
        g = jnp.minimum(g, SWIGLU_LIMIT)
        u = jnp.clip(u, -SWIGLU_LIMIT, SWIGLU_LIMIT)
        act = (u + 1.0) * g * (1.0 / (1.0 + jnp.exp(-SWIGLU_ALPHA * g)))
        y = jnp.dot(act.astype(BF16), wd_bf[...], preferred_element_type=F32) + bd_ref[...]
        _store_token_tiles(ys_ref, y)

    @pl.when(tv_ref[i] == 0)
    def _():
        ys_ref[...] = jnp.zeros_like(ys_ref)


def _experts(xs, plan, wg, bg, wu, bu, wd, bd, tm=MOE_TM):
    n_tiles = xs.shape[0] // (tm * TOK_ROWS)
    d, f = wg.shape[-2:]
    assert d == f == D_MODEL
    b_spec = lambda n: pl.BlockSpec((None, 1, n), lambda i, te, *_: (te[i], 0, 0))
    rows = pl.BlockSpec((tm * TOK_ROWS, LANES), lambda i, *_: (i, 0))
    hbm = pl.BlockSpec(memory_space=pl.ANY)
    return pl.pallas_call(
        functools.partial(_expert_kernel, tm=tm),
        grid_spec=pltpu.PrefetchScalarGridSpec(
            num_scalar_prefetch=4,
            grid=(n_tiles,),
            in_specs=[rows, hbm, b_spec(f), hbm, b_spec(f), hbm, b_spec(d)],
            out_specs=rows,
            scratch_shapes=[pltpu.VMEM((2, 3, d, f), F32),
                            pltpu.VMEM((d, f), BF16), pltpu.VMEM((d, f), BF16), pltpu.VMEM((f, d), BF16),
                            pltpu.SemaphoreType.DMA((2, 3))],
        ),
        out_shape=jax.ShapeDtypeStruct(xs.shape, F32),
        compiler_params=_cparams(1),
        name="moe_experts",
    )(*plan, xs, wg, bg.reshape(N_EXPERTS, 1, f), wu, bu.reshape(N_EXPERTS, 1, f),
      wd, bd.reshape(N_EXPERTS, 1, d))


def _combine_kernel(dcur_ref, dnxt_ref, ys_ref, x1_ref, gate_ref, gt2_ref, gfin_ref, o_ref,
                    buf, sems, *, tm):
    i = pl.program_id(0)
    cur = i & 1
    slot_toks = TOP_K * tm

    def make(slot):
        def f(n, t, d):
            return pltpu.make_async_copy(_tok(ys_ref, d), _tok(buf, slot * slot_toks + t * tm + n),
                                         sems.at[slot])
        return f

    @pl.when(i == 0)
    def _():
        _start_rows(dcur_ref, tm, make(cur))

    @pl.when(i + 1 < pl.num_programs(0))
    def _():
        _start_rows(dnxt_ref, tm, make(1 - cur))

    _wait_rows(dcur_ref, tm, make(cur))
    base = pl.multiple_of(cur * slot_toks * TOK_ROWS, slot_toks * TOK_ROWS)
    gate = gate_ref[...]
    y = gate[:, 0:1] * _load_token_tiles(buf, tm, base=base)
    for t in range(1, TOP_K):
        y = y + gate[:, t:t + 1] * _load_token_tiles(buf, tm, base=base + t * tm * TOK_ROWS)
    x2 = x1_ref[...] + gt2_ref[...] * y
    o_ref[...] = _rmsnorm(x2, gfin_ref[...])


def _combine(ys, dest, x1, gate, gt2, gfin, seq, tm=TOK_TM):
    d = D_MODEL
    n = x1.shape[0]
    tm = min(tm, n)
    nt = n // tm
    row = lambda w: pl.BlockSpec((tm, w), lambda i: (i, 0))
    if seq:
        mod = pl.BlockSpec((None, 1, d), lambda i: (i // (seq // tm), 0, 0))
    else:
        mod = row(d)
    dspec = lambda fn: pl.BlockSpec((None, 1, tm * TOP_K), lambda i: (fn(i), 0, 0), memory_space=pltpu.SMEM)
    dest3 = dest.reshape(nt, 1, tm * TOP_K)
    return pl.pallas_call(
        functools.partial(_combine_kernel, tm=tm),
        grid=(nt,),
        in_specs=[dspec(lambda i: i), dspec(lambda i: jnp.minimum(i + 1, nt - 1)),
                  pl.BlockSpec(memory_space=pl.ANY), row(d), row(LANES), mod,
                  pl.BlockSpec((1, d), lambda i: (0, 0))],
        out_specs=row(d),
        out_shape=jax.ShapeDtypeStruct(x1.shape, F32),
        scratch_shapes=[pltpu.VMEM((2 * TOP_K * tm * TOK_ROWS, LANES), F32),
                        pltpu.SemaphoreType.DMA((2,))],
        compiler_params=_cparams(1),
        name="moe_combine",
    )(dest3, dest3, ys, x1, gate, gt2, gfin)


def _routing_plan(counts, n_pairs, tm=MOE_TM):
    pc = (counts + tm - 1) // tm * tm
    pend = jnp.cumsum(pc)
    pstart = pend - pc
    n_rows = -(-(n_pairs + N_EXPERTS * (tm - 1)) // tm) * tm
    n_tiles = n_rows // tm
    tile_row = jnp.arange(n_tiles, dtype=I32) * tm
    last_used = jnp.max(jnp.where(pc > 0, jnp.arange(N_EXPERTS, dtype=I32), 0))
    tile_e = jnp.minimum(jnp.sum((tile_row[:, None] >= pend[None, :]).astype(I32), axis=1), last_used)
    tile_valid = (tile_row < pend[-1]).astype(I32)
    ids = jnp.arange(N_EXPERTS, dtype=I32)
    used = pc > 0
    slot_e = (jnp.cumsum(used.astype(I32)) - 1) & 1
    later = jnp.where(used[None, :] & (ids[None, :] > ids[:, None]), ids[None, :], N_EXPERTS)
    next_e = jnp.min(later, axis=1)
    next_e = jnp.where(next_e == N_EXPERTS, -1, next_e)
    pick = (tile_e[:, None] == ids[None, :]).astype(I32)
    plan = (tile_e, tile_valid, jnp.sum(pick * next_e[None, :], axis=1),
            jnp.sum(pick * slot_e[None, :], axis=1))
    pad = ((pstart + counts).astype(I32), (pc - counts).astype(I32),
           (pend[-1:] // tm).astype(I32))
    return pstart.astype(I32), plan, n_rows, pad


def _rope_tables(pos):
    half = HEAD_DIM // 2
    inv = ROPE_THETA ** (-np.arange(half, dtype=np.float64) / half)
    ang = np.asarray(pos, np.float64)[:, None] * inv[None, :]
    cos = np.cos(ang)
    sin = np.sin(ang)
    cosf = np.concatenate([cos, cos, cos, cos], axis=1).astype(np.float32)
    sinf = np.concatenate([-sin, sin, -sin, sin], axis=1).astype(np.float32)
    return jnp.asarray(cosf), jnp.asarray(sinf)


def kernel(x_prompt, x_sample, cache_k, cache_v, state_conv, c_prompt, c_sample, w_ada, b_ada,
           g_norm_mix, w_in, conv_w, g_attn_out, g_conv_out, w_out, g_norm_ffn, w_router, b_router,
           w_gate, b_gate, w_up, b_up, w_down, b_down, g_final):
    depth = w_in.shape[0]
    assert depth == 1, "single-layer trunk"
    bp, sp, d = x_prompt.shape
    bs, ts, _ = x_sample.shape
    ns = bs * ts
    n_cache = cache_k.shape[2]
    l = 0

    mods = _modulations(jnp.concatenate([c_prompt, c_sample], axis=0), w_ada[l], b_ada[l])
    mp = [m.reshape(bp, 1, d) for m in jnp.split(mods[:bp], 6, axis=-1)]
    ms = [jnp.repeat(m, ts, axis=0) for m in jnp.split(mods[bp:], 6, axis=-1)]

    w_in_bf = w_in[l].astype(BF16)
    w_out_bf = w_out[l].astype(BF16)
    wr_pad = jnp.pad(w_router[l], ((0, 0), (0, LANES - N_EXPERTS)))
    br_pad = jnp.pad(b_router[l].reshape(1, N_EXPERTS), ((0, 0), (0, LANES - N_EXPERTS)),
                     constant_values=NEG)
    g_mix = g_norm_mix[l].reshape(1, d)
    g_ffn = g_norm_ffn[l].reshape(1, d)
    g_att = g_attn_out[l].reshape(1, ATT_W)
    g_cnv = g_conv_out[l].reshape(1, CONV_W)
    g_fin = g_final.reshape(1, d)

    cos_p, sin_p = _rope_tables(np.arange(sp))
    q_p, k_p, v_p, conv_p, tail_p = _inproj_prompt(x_prompt, mp[0], mp[1], g_mix, w_in_bf,
                                                   cos_p, sin_p, conv_w[l])
    att_p = _attn_prompt(q_p, k_p, v_p)

    xs_rows = x_sample.reshape(ns, d)
    cos_s, sin_s = _rope_tables(np.tile(PAST_LEN + np.arange(ts), bs))
    st = state_conv[l]
    zrow = jnp.zeros((bs, 1, CONV_W), F32)
    s1 = jnp.concatenate([st[:, 1:2], zrow, zrow, zrow], axis=1).reshape(ns, CONV_W)
    s2 = jnp.concatenate([st[:, 0:1], st[:, 1:2], zrow, zrow], axis=1).reshape(ns, CONV_W)
    q_s, k_s, v_s, conv_s, cu_s = _inproj_sample(xs_rows, ms[0], ms[1], g_mix, w_in_bf,
                                                 cos_s, sin_s, conv_w[l], s1, s2, ts)
    rows8 = lambda a: jnp.pad(a.reshape(bs, ts, ATT_W), ((0, 0), (0, 8 - ts), (0, 0)))
    kt = jnp.transpose(cache_k[l], (0, 2, 3, 1))
    vt = jnp.transpose(cache_v[l], (0, 2, 3, 1))
    att_s = _attn_sample(rows8(q_s), kt, vt, rows8(k_s), rows8(v_s), ts)[:, :ts].reshape(ns, ATT_W)

    cnt0 = jnp.zeros((8, LANES), F32)
    x1_p, h2_p, idx_p, gate_p, rank_p, cnt_p = _merge(
        att_p, conv_p, x_prompt, mp[2], mp[3], mp[4], g_att, g_cnv, g_ffn,
        w_out_bf, wr_pad, br_pad, cnt0, 2 * TOK_TM)
    x1_s, h2_s, idx_s, gate_s, rank_s, cnt_all = _merge(
        att_s, conv_s, xs_rows, ms[2], ms[3], ms[4], g_att, g_cnv, g_ffn,
        w_out_bf, wr_pad, br_pad, cnt_p, ns)

    n_tok = bp * sp + ns
    counts = cnt_all[0, :N_EXPERTS].astype(I32)
    pstart, plan, n_rows, pad = _routing_plan(counts, n_tok * TOP_K)
    ids = jnp.arange(N_EXPERTS, dtype=I32)
    slot_of = lambda idx, rank: jnp.sum(
        jnp.where(idx[..., :TOP_K, None] == ids, pstart, 0), axis=-1) + rank[..., :TOP_K]
    dest_p = slot_of(idx_p, rank_p).reshape(bp * sp, TOP_K)
    dest_s = slot_of(idx_s, rank_s)
    xs_sorted = _dispatch(h2_p.reshape(bp * sp * TOK_ROWS, LANES), dest_p, h2_s, dest_s, pad, n_rows)
    ys = _experts(xs_sorted, plan, w_gate[l], b_gate[l], w_up[l], b_up[l], w_down[l], b_down[l])
    y_prompt = _combine(ys, dest_p, x1_p.reshape(bp * sp, d), gate_p.reshape(bp * sp, LANES),
                        mp[5], g_fin, sp).reshape(bp, sp, d)
    y_sample = _combine(ys, dest_s, x1_s, gate_s, ms[5], g_fin, 0).reshape(bs, ts, d)

    heads = lambda a, b, s: a.reshape(1, b, s, N_HEADS, HEAD_DIM)
    keep = min(WINDOW_MAX, sp)
    return (y_prompt, y_sample,
            heads(k_p, bp, sp)[:, :, sp - keep:], heads(v_p, bp, sp)[:, :, sp - keep:],
            tail_p[:, 8 - (CONV_K - 1):][None],
            heads(k_s, bs, ts), heads(v_s, bs, ts),
            cu_s.reshape(bs, ts, CONV_W)[:, ts - (CONV_K - 1):][None])
```

```python
import functools

import jax
import jax.numpy as jnp
import numpy as np
from jax import lax
from jax.experimental import pallas as pl
from jax.experimental.pallas import tpu as pltpu

F32 = jnp.float32
BF16 = jnp.bfloat16
I32 = jnp.int32
HIGHEST = lax.Precision.HIGHEST

D_MODEL = 1024
HEAD_DIM = 64
N_HEADS = 12
ATT_W = N_HEADS * HEAD_DIM
CONV_W = D_MODEL - ATT_W
CONV_K = 3
PATTERNS = ((128, 1), (512, 4), (2048, 16))
WINDOW_MAX = 2048
PAST_LEN = 16384
ROPE_THETA = 10000.0
N_EXPERTS = 32
TOP_K = 4
SWIGLU_ALPHA = 1.702
SWIGLU_LIMIT = 7.0
NORM_EPS = 1e-6
NEG = -1e30
IN_W = 3 * ATT_W + 3 * CONV_W

LANES = 128
Q_BLK = 128
NEAR_W = 512
FAR_D = 16
MOE_TM = 256
TOK_TM = 256
VMEM_LIMIT = 56 * 1024 * 1024
TOK_ROWS = D_MODEL // LANES


def _cparams(n_axes, vmem=VMEM_LIMIT):
    return pltpu.CompilerParams(dimension_semantics=("arbitrary",) * n_axes,
                                vmem_limit_bytes=vmem)


def _multiplicity(delta):
    delta = np.asarray(delta)
    c = np.zeros(delta.shape, np.float32)
    for w, d in PATTERNS:
        c += ((delta >= 0) & (delta <= w) & (delta % d == 0)).astype(np.float32)
    return c


def _ada_kernel(c_ref, w_ref, b_ref, o_ref):
    c = c_ref[...]
    s = c / (1.0 + jnp.exp(-c))
    o_ref[...] = jnp.dot(s, w_ref[...], precision=HIGHEST,
                         preferred_element_type=F32) + b_ref[...]


def _modulations(c_all, w_ada, b_ada):
    r, d = c_all.shape
    n = w_ada.shape[1]
    tn = 1536
    return pl.pallas_call(
        _ada_kernel,
        grid=(n // tn,),
        in_specs=[pl.BlockSpec((r, d), lambda j: (0, 0)),
                  pl.BlockSpec((d, tn), lambda j: (0, j)),
                  pl.BlockSpec((1, tn), lambda j: (0, j))],
        out_specs=pl.BlockSpec((r, tn), lambda j: (0, j)),
        out_shape=jax.ShapeDtypeStruct((r, n), F32),
        compiler_params=_cparams(1),
        name="ada_modulation",
    )(c_all, w_ada, b_ada.reshape(1, n))


def _norm_mod(x, g, shift, scale):
    ms = jnp.mean(x * x, axis=-1, keepdims=True)
    return (x * lax.rsqrt(ms + NORM_EPS) * g) * (1.0 + scale) + shift


def _rmsnorm(x, g):
    ms = jnp.mean(x * x, axis=-1, keepdims=True)
    return x * lax.rsqrt(ms + NORM_EPS) * g


def _store_token_tiles(ref, x, base=0):
    tm = x.shape[0]
    for c in range(TOK_ROWS):
        ref[pl.ds(base + c, tm, stride=TOK_ROWS), :] = x[:, c * LANES:(c + 1) * LANES]


def _load_token_tiles(ref, tm, base=0):
    return jnp.concatenate([ref[pl.ds(base + c, tm, stride=TOK_ROWS), :] for c in range(TOK_ROWS)],
                           axis=1)


def _swap_halves(xc):
    lane = lax.broadcasted_iota(I32, xc.shape, 1)
    first = (lane & (HEAD_DIM - 1)) < HEAD_DIM // 2
    return jnp.where(first, pltpu.roll(xc, LANES - HEAD_DIM // 2, 1),
                     pltpu.roll(xc, HEAD_DIM // 2, 1))


def _rope(x, cosf, sinf):
    outs = []
    for c in range(x.shape[1] // LANES):
        xc = x[:, c * LANES:(c + 1) * LANES]
        outs.append(xc * cosf + _swap_halves(xc) * sinf)
    return jnp.concatenate(outs, axis=1)


def _inproj_kernel(*refs, tm, sample, seq_per_batch):
    if sample:
        (x_ref, sh_ref, sc_ref, g_ref, w_ref, cos_ref, sin_ref, cw_ref, s1_ref, s2_ref,
         q_ref, k_ref, v_ref, conv_ref, cu_ref, cu_ext) = refs
    else:
        (x_ref, sh_ref, sc_ref, g_ref, w_ref, cos_ref, sin_ref, cw_ref,
         q_ref, k_ref, v_ref, conv_ref, tail_ref, cu_ext) = refs
    h = _norm_mod(x_ref[...], g_ref[...], sh_ref[...], sc_ref[...])
    z = jnp.dot(h.astype(BF16), w_ref[...], preferred_element_type=F32)
    cosf = cos_ref[...]
    sinf = sin_ref[...]
    q_ref[...] = _rope(z[:, 0:ATT_W], cosf, sinf) * (HEAD_DIM ** -0.5)
    k_ref[...] = _rope(z[:, ATT_W:2 * ATT_W], cosf, sinf)
    v_ref[...] = z[:, 2 * ATT_W:3 * ATT_W]
    o = 3 * ATT_W
    gb = z[:, o:o + CONV_W]
    cu = z[:, o + CONV_W:o + 2 * CONV_W] * z[:, o + 2 * CONV_W:o + 3 * CONV_W]
    if sample:
        cu_ext[0:8, :] = jnp.zeros((8, CONV_W), F32)
    else:
        @pl.when(pl.program_id(1) == 0)
        def _():
            cu_ext[0:8, :] = jnp.zeros((8, CONV_W), F32)
    cu_ext[8:8 + tm, :] = cu
    p1 = cu_ext[7:7 + tm, :]
    p2 = cu_ext[6:6 + tm, :]
    if sample:
        t = lax.broadcasted_iota(I32, (tm, CONV_W), 0) % seq_per_batch
        p1 = jnp.where(t >= 1, p1, 0.0) + s1_ref[...]
        p2 = jnp.where(t >= 2, p2, 0.0) + s2_ref[...]
        cu_ref[...] = cu
    cw = cw_ref[...]
    conv_ref[...] = gb * (cw[0:1, :] * p2 + cw[1:2, :] * p1 + cw[2:3, :] * cu)
    if not sample:
        tail = cu_ext[tm:tm + 8, :]
        tail_ref[...] = tail
        cu_ext[0:8, :] = tail


def _inproj_prompt(x, shift, scale, g, w_bf, cosf, sinf, conv_w, tm=512):
    b, s, d = x.shape
    row = lambda bi, j: (bi, j, 0)
    per_b = lambda bi, j: (bi, 0, 0)
    const = lambda bi, j: (0, 0)
    outs = pl.pallas_call(
        functools.partial(_inproj_kernel, tm=tm, sample=False, seq_per_batch=s),
        grid=(b, s // tm),
        in_specs=[pl.BlockSpec((None, tm, d), row),
                  pl.BlockSpec((None, 1, d), per_b),
                  pl.BlockSpec((None, 1, d), per_b),
                  pl.BlockSpec((1, d), const),
                  pl.BlockSpec((d, IN_W), const),
                  pl.BlockSpec((tm, LANES), lambda bi, j: (j, 0)),
                  pl.BlockSpec((tm, LANES), lambda bi, j: (j, 0)),
                  pl.BlockSpec((CONV_K, CONV_W), const)],
        out_specs=[pl.BlockSpec((None, tm, ATT_W), row),
                   pl.BlockSpec((None, tm, ATT_W), row),
                   pl.BlockSpec((None, tm, ATT_W), row),
                   pl.BlockSpec((None, tm, CONV_W), row),
                   pl.BlockSpec((None, 8, CONV_W), per_b)],
        out_shape=[jax.ShapeDtypeStruct((b, s, ATT_W), F32),
                   jax.ShapeDtypeStruct((b, s, ATT_W), F32),
                   jax.ShapeDtypeStruct((b, s, ATT_W), F32),
                   jax.ShapeDtypeStruct((b, s, CONV_W), F32),
                   jax.ShapeDtypeStruct((b, 8, CONV_W), F32)],
        scratch_shapes=[pltpu.VMEM((tm + 8, CONV_W), F32)],
        compiler_params=_cparams(2),
        name="inproj_prompt",
    )(x, shift, scale, g, w_bf, cosf, sinf, conv_w)
    return outs


def _inproj_sample(x, shift, scale, g, w_bf, cosf, sinf, conv_w, s1, s2, seq_per_batch):
    n, d = x.shape
    full = lambda shape: pl.BlockSpec(shape, lambda i: (0, 0))
    outs = pl.pallas_call(
        functools.partial(_inproj_kernel, tm=n, sample=True, seq_per_batch=seq_per_batch),
        grid=(1,),
        in_specs=[full((n, d)), full((n, d)), full((n, d)), full((1, d)), full((d, IN_W)),
                  full((n, LANES)), full((n, LANES)), full((CONV_K, CONV_W)),
                  full((n, CONV_W)), full((n, CONV_W))],
        out_specs=[full((n, ATT_W)), full((n, ATT_W)), full((n, ATT_W)),
                   full((n, CONV_W)), full((n, CONV_W))],
        out_shape=[jax.ShapeDtypeStruct((n, ATT_W), F32)] * 3
        + [jax.ShapeDtypeStruct((n, CONV_W), F32)] * 2,
        scratch_shapes=[pltpu.VMEM((n + 8, CONV_W), F32)],
        compiler_params=_cparams(1),
        name="inproj_sample",
    )(x, shift, scale, g, w_bf, cosf, sinf, conv_w, s1, s2)
    return outs


def _stack_heads(x):
    lo = lax.broadcasted_iota(I32, x.shape, 1) < HEAD_DIM
    zero = jnp.zeros_like(x)
    return jnp.concatenate([jnp.where(lo, x, zero), jnp.where(lo, zero, x)], axis=0)


def _split_pv(r0, r1, m, rows):
    lo = lax.broadcasted_iota(I32, r0.shape, 1) < HEAD_DIM
    num = jnp.where(lo, r0, r1)
    den = jnp.where(lo, pltpu.roll(r0, HEAD_DIM, 1), pltpu.roll(r1, HEAD_DIM, 1))
    mx = jnp.where(lo, jnp.broadcast_to(m[:rows], r0.shape), jnp.broadcast_to(m[rows:], r0.shape))
    return num, den, mx


def _attn_prompt_kernel(q_ref, k_ref, v_ref, lmult_ref, o_ref,
                        kpad, v0pad, v1pad, num3, den3, max3,
                        *, seq, far_unroll, near_unroll, nq, far_s):
    nt = (((1,), (1,)), ((), ()))
    n_blk = seq // Q_BLK
    lo1 = lax.broadcasted_iota(I32, (Q_BLK, LANES), 1) < HEAD_DIM
    ones = jnp.ones((Q_BLK, LANES), F32)

    zpad = jnp.zeros((NEAR_W, LANES), BF16)
    kpad[0:NEAR_W, :] = zpad
    v0pad[0:NEAR_W, :] = zpad
    v1pad[0:NEAR_W, :] = zpad

    def fill(i, c):
        s0 = pl.multiple_of(i * Q_BLK, Q_BLK)
        d0 = pl.multiple_of(i * Q_BLK + NEAR_W, Q_BLK)
        kpad[pl.ds(d0, Q_BLK), :] = k_ref[pl.ds(s0, Q_BLK), :].astype(BF16)
        vb = v_ref[pl.ds(s0, Q_BLK), :]
        v0pad[pl.ds(d0, Q_BLK), :] = jnp.where(lo1, vb, ones).astype(BF16)
        v1pad[pl.ds(d0, Q_BLK), :] = jnp.where(lo1, ones, vb).astype(BF16)
        return c
    lax.fori_loop(0, n_blk, fill, 0)

    nf = seq // far_s
    row = lax.broadcasted_iota(I32, (2 * nf, nf), 0) & (nf - 1)
    col = lax.broadcasted_iota(I32, (2 * nf, nf), 1)
    allowed = jnp.logical_and(col <= row, ((row - col) & (FAR_D // far_s - 1)) == 0)
    lof = lax.broadcasted_iota(I32, (nf, LANES), 1) < HEAD_DIM
    onef = jnp.ones((nf, LANES), F32)

    def far(r, c):
        sl = pl.ds(r, nf, stride=far_s)
        q2 = _stack_heads(q_ref[sl, :]).astype(BF16)
        kr = k_ref[sl, :].astype(BF16)
        vr = v_ref[sl, :]
        s = lax.dot_general(q2, kr, nt, preferred_element_type=F32)
        s = jnp.where(allowed, s, NEG)
        m = jnp.max(s, axis=1, keepdims=True)
        p = jnp.exp(s - m).astype(BF16)
        r0 = jnp.dot(p[:nf], jnp.where(lof, vr, onef).astype(BF16), preferred_element_type=F32)
        r1 = jnp.dot(p[nf:], jnp.where(lof, onef, vr).astype(BF16), preferred_element_type=F32)
        num, den, mx = _split_pv(r0, r1, m, nf)
        num3[sl, :] = num
        den3[sl, :] = den
        max3[sl, :] = mx
        return c
    lax.fori_loop(0, far_s, far, 0, unroll=far_unroll)

    nk = NEAR_W + nq
    kcol = lax.broadcasted_iota(I32, (1, nk), 1)

    def near(first_blocks):
        def body(i, c):
            s0 = pl.multiple_of(i * nq, nq)
            q2 = _stack_heads(q_ref[pl.ds(s0, nq), :]).astype(BF16)
            kw = kpad[pl.ds(s0, nk), :]
            s = lax.dot_general(q2, kw, nt, preferred_element_type=F32) + lmult_ref[...]
            if first_blocks:
                s = jnp.where(kcol >= NEAR_W - s0, s, NEG)
            m = jnp.max(s, axis=1, keepdims=True)
            p = jnp.exp(s - m).astype(BF16)
            r0 = jnp.dot(p[:nq], v0pad[pl.ds(s0, nk), :], preferred_element_type=F32)
            r1 = jnp.dot(p[nq:], v1pad[pl.ds(s0, nk), :], preferred_element_type=F32)
            num, den, mx = _split_pv(r0, r1, m, nq)
            mx3 = max3[pl.ds(s0, nq), :]
            mm = jnp.maximum(mx, mx3)
            a = jnp.exp(mx - mm)
            b = jnp.exp(mx3 - mm)
            o_ref[pl.ds(s0, nq), :] = ((num * a + num3[pl.ds(s0, nq), :] * b)
                                       / (den * a + den3[pl.ds(s0, nq), :] * b))
            return c
        return body

    n_first = NEAR_W // nq
    lax.fori_loop(0, n_first, near(True), 0, unroll=near_unroll)
    lax.fori_loop(n_first, seq // nq, near(False), 0, unroll=near_unroll)


def _near_table(nq):
    i = np.arange(nq)[:, None]
    kl = np.arange(NEAR_W + nq)[None, :]
    delta = i + NEAR_W - kl
    mult = np.zeros(delta.shape, np.float32)
    for w, d in PATTERNS[:2]:
        mult += ((delta >= 0) & (delta <= w) & (delta % d == 0)).astype(np.float32)
    lmult = np.where(mult > 0, np.log(np.maximum(mult, 1.0)), NEG).astype(np.float32)
    return np.tile(lmult, (2, 1))


def _attn_prompt(q, k, v, far_unroll=4, near_unroll=3, nq=2 * Q_BLK, far_s=FAR_D):
    b, s, _ = q.shape
    lmult = _near_table(nq)
    blk = pl.BlockSpec((None, s, LANES), lambda bi, hp: (bi, 0, hp))
    tab = pl.BlockSpec((2 * nq, NEAR_W + nq), lambda bi, hp: (0, 0))
    return pl.pallas_call(
        functools.partial(_attn_prompt_kernel, seq=s, far_unroll=far_unroll, near_unroll=near_unroll,
                          nq=nq, far_s=far_s),
        grid=(b, ATT_W // LANES),
        in_specs=[blk, blk, blk, tab],
        out_specs=blk,
        out_shape=jax.ShapeDtypeStruct((b, s, ATT_W), F32),
        scratch_shapes=[pltpu.VMEM((s + NEAR_W, LANES), BF16)] * 3
        + [pltpu.VMEM((s, LANES), F32)] * 3,
        compiler_params=_cparams(2),
        name="attn_prompt",
    )(q, k, v, jnp.asarray(lmult))


def _attn_sample_kernel(q_ref, kt_ref, vt_ref, kn_ref, vn_ref, bias_ref, mult_ref, bn_ref, mn_ref,
                        o_ref):
    nt = (((1,), (1,)), ((), ()))
    q = q_ref[...]
    kn = kn_ref[...]
    vn = vn_ref[...]
    for h in range(N_HEADS):
        hs = slice(h * HEAD_DIM, (h + 1) * HEAD_DIM)
        qh = q[:, hs].astype(BF16)
        s_c = jnp.dot(qh, kt_ref[h].astype(BF16), preferred_element_type=F32) + bias_ref[...]
        s_n = lax.dot_general(qh, kn[:, hs].astype(BF16), nt, preferred_element_type=F32) + bn_ref[...]
        m = jnp.maximum(jnp.max(s_c, axis=1, keepdims=True), jnp.max(s_n, axis=1, keepdims=True))
        p_c = jnp.exp(s_c - m) * mult_ref[...]
        p_n = jnp.exp(s_n - m) * mn_ref[...]
        den = jnp.sum(p_c, axis=1, keepdims=True) + jnp.sum(p_n, axis=1, keepdims=True)
        num = (lax.dot_general(p_c.astype(BF16), vt_ref[h].astype(BF16), nt, preferred_element_type=F32)
               + jnp.dot(p_n.astype(BF16), vn[:, hs].astype(BF16), preferred_element_type=F32))
        o_ref[:, hs] = num / den


def _sample_tables(n_cache, t_new):
    t = np.arange(8)[:, None] % t_new
    rho = np.arange(n_cache)[None, :]
    c_cache = _multiplicity(t + n_cache - rho).astype(np.float32)
    tp = np.arange(8)[None, :]
    c_new = np.where(tp < t_new, _multiplicity(t - tp), 0.0).astype(np.float32)
    bias = lambda c: np.where(c > 0, 0.0, NEG).astype(np.float32)
    return bias(c_cache), c_cache, bias(c_new), c_new


def _attn_sample(q8, kt, vt, kn8, vn8, t_new):
    b = q8.shape[0]
    n_c = kt.shape[-1]
    tables = [jnp.asarray(a) for a in _sample_tables(n_c, t_new)]
    cache = pl.BlockSpec((None, N_HEADS, HEAD_DIM, n_c), lambda i: (i, 0, 0, 0))
    row = pl.BlockSpec((None, 8, ATT_W), lambda i: (i, 0, 0))
    tab = pl.BlockSpec((8, n_c), lambda i: (0, 0))
    tab_new = pl.BlockSpec((8, 8), lambda i: (0, 0))
    return pl.pallas_call(
        _attn_sample_kernel,
        grid=(b,),
        in_specs=[row, cache, cache, row, row, tab, tab, tab_new, tab_new],
        out_specs=row,
        out_shape=jax.ShapeDtypeStruct((b, 8, ATT_W), F32),
        compiler_params=_cparams(1),
        name="attn_sample",
    )(q8, kt, vt, kn8, vn8, *tables)


def _merge_kernel(att_ref, conv_ref, x_ref, gt1_ref, sh2_ref, sc2_ref, ga_ref, gc_ref, gf_ref,
                  wo_ref, wr_ref, wrl_ref, br_ref, cnt0_ref, tri_ref,
                  x1_ref, h2_ref, idx_ref, gate_ref, rank_ref, cnt_ref, cnt_sc, *, n_axes):
    first = pl.program_id(0) == 0
    if n_axes == 2:
        first = jnp.logical_and(first, pl.program_id(1) == 0)

    @pl.when(first)
    def _():
        cnt_sc[...] = cnt0_ref[...]

    an = _rmsnorm(att_ref[...], ga_ref[...]).astype(BF16)
    cn = _rmsnorm(conv_ref[...], gc_ref[...]).astype(BF16)
    mix = (jnp.dot(an, wo_ref[0:ATT_W, :], preferred_element_type=F32)
           + jnp.dot(cn, wo_ref[ATT_W:D_MODEL, :], preferred_element_type=F32))
    x1 = x_ref[...] + gt1_ref[...] * mix
    x1_ref[...] = x1
    h2 = _norm_mod(x1, gf_ref[...], sh2_ref[...], sc2_ref[...])
    _store_token_tiles(h2_ref, h2)
    hi = h2.astype(BF16)
    lo = (h2 - hi.astype(F32)).astype(BF16)
    logits = (jnp.dot(hi, wr_ref[...], preferred_element_type=F32)
              + (jnp.dot(hi, wrl_ref[...], preferred_element_type=F32)
                 + jnp.dot(lo, wr_ref[...], preferred_element_type=F32))) + br_ref[...]
    tm = logits.shape[0]
    lane = lax.broadcasted_iota(I32, (tm, LANES), 1)
    work = logits
    vals, idxs = [], []
    for _ in range(TOP_K):
        mx = jnp.max(work, axis=1, keepdims=True)
        ix = jnp.min(jnp.where(work == mx, lane, LANES), axis=1, keepdims=True)
        vals.append(mx)
        idxs.append(ix)
        work = jnp.where(lane == ix, 3.0 * NEG, work)
    es = [jnp.exp(v - vals[0]) for v in vals]
    den = es[0] + es[1] + es[2] + es[3]
    onehot = jnp.zeros((tm, LANES), F32)
    for ix in idxs:
        onehot = onehot + (lane == ix).astype(F32)
    before = jnp.dot(tri_ref[...], onehot.astype(BF16), preferred_element_type=F32) + cnt_sc[0:1, :]
    idx_o = jnp.zeros((tm, LANES), I32)
    gate_o = jnp.zeros((tm, LANES), F32)
    rank_o = jnp.zeros((tm, LANES), F32)
    for t in range(TOP_K):
        rk = jnp.sum(jnp.where(lane == idxs[t], before, 0.0), axis=1, keepdims=True)
        idx_o = jnp.where(lane == t, idxs[t], idx_o)
        gate_o = jnp.where(lane == t, es[t] / den, gate_o)
        rank_o = jnp.where(lane == t, rk, rank_o)
    idx_ref[...] = idx_o
    gate_ref[...] = gate_o
    rank_ref[...] = rank_o.astype(I32)
    cnt_sc[...] = cnt_sc[...] + jnp.sum(onehot, axis=0, keepdims=True)
    cnt_ref[...] = cnt_sc[...]


def _merge(att, conv, x, gt1, sh2, sc2, ga, gc, gf, wo_bf, wr_pad, br_pad, cnt0, tm):
    d = D_MODEL
    wr_hi = wr_pad.astype(BF16)
    wr_lo = (wr_pad - wr_hi.astype(F32)).astype(BF16)
    tri = jnp.asarray(np.tril(np.ones((tm, tm), np.float32), -1), BF16)
    if att.ndim == 3:
        b, s, _ = att.shape
        grid = (b, s // tm)
        row = lambda w: pl.BlockSpec((None, tm, w), lambda bi, j: (bi, j, 0))
        mod = pl.BlockSpec((None, 1, d), lambda bi, j: (bi, 0, 0))
        const = lambda shape: pl.BlockSpec(shape, lambda bi, j: (0, 0))
        lead = (b, s)
        tiles = pl.BlockSpec((None, tm * TOK_ROWS, LANES), lambda bi, j: (bi, j, 0))
        tiles_shape = (b, s * TOK_ROWS, LANES)
    else:
        n = att.shape[0]
        grid = (n // tm,)
        row = lambda w: pl.BlockSpec((tm, w), lambda i: (i, 0))
        mod = row(d)
        const = lambda shape: pl.BlockSpec(shape, lambda i: (0, 0))
        lead = (n,)
        tiles = pl.BlockSpec((tm * TOK_ROWS, LANES), lambda i: (i, 0))
        tiles_shape = (n * TOK_ROWS, LANES)
    return pl.pallas_call(
        functools.partial(_merge_kernel, n_axes=len(grid)),
        grid=grid,
        in_specs=[row(ATT_W), row(CONV_W), row(d), mod, mod, mod,
                  const((1, ATT_W)), const((1, CONV_W)), const((1, d)),
                  const((d, d)), const((d, LANES)), const((d, LANES)), const((1, LANES)),
                  const((8, LANES)), const((tm, tm))],
        out_specs=[row(d), tiles, row(LANES), row(LANES), row(LANES), const((8, LANES))],
        out_shape=[jax.ShapeDtypeStruct(lead + (d,), F32),
                   jax.ShapeDtypeStruct(tiles_shape, F32),
                   jax.ShapeDtypeStruct(lead + (LANES,), I32),
                   jax.ShapeDtypeStruct(lead + (LANES,), F32),
                   jax.ShapeDtypeStruct(lead + (LANES,), I32),
                   jax.ShapeDtypeStruct((8, LANES), F32)],
        scratch_shapes=[pltpu.VMEM((8, LANES), F32)],
        compiler_params=_cparams(len(grid)),
        name="merge_route",
    )(att, conv, x, gt1, sh2, sc2, ga, gc, gf, wo_bf, wr_hi, wr_lo, br_pad, cnt0, tri)


def _tok(ref, n, count=1):
    return ref.at[pl.ds(pl.multiple_of(n * TOK_ROWS, TOK_ROWS), count * TOK_ROWS)]


ROW_DMA_UNROLL = 4


def _start_rows(dest_ref, tm, make):
    def start(n, c):
        for t in range(TOP_K):
            make(n, t, dest_ref[0, n * TOP_K + t]).start(priority=t % 2)
        return c
    lax.fori_loop(0, tm, start, 0, unroll=ROW_DMA_UNROLL)


def _wait_rows(dest_ref, tm, make):
    def wait(n, c):
        for t in range(TOP_K):
            make(n, t, dest_ref[0, n * TOP_K + t]).wait()
        return c
    lax.fori_loop(0, tm, wait, 0, unroll=2 * ROW_DMA_UNROLL)


def _row_copies(dest_ref, tm, make):
    _start_rows(dest_ref, tm, make)
    _wait_rows(dest_ref, tm, make)


def _dispatch_kernel(ps_ref, pn_ref, t0_ref, dest_ref, h_ref, dest_s_ref, hs_ref, xs_ref,
                     zbuf, sem, *, tm, n_s, row_tm, n_tiles):
    def make(n, t, d):
        return pltpu.make_async_copy(_tok(h_ref, n), _tok(xs_ref, d), sem)
    _row_copies(dest_ref, tm, make)

    @pl.when(pl.program_id(0) == pl.num_programs(0) - 1)
    def _():
        def make_s(n, t, d):
            return pltpu.make_async_copy(_tok(hs_ref, n), _tok(xs_ref, d), sem)
        _row_copies(dest_s_ref, n_s, make_s)
        _zero_fill(ps_ref, pn_ref, t0_ref, xs_ref, zbuf, sem, row_tm, n_tiles)


def _dispatch(h2, dest, h2_s, dest_s, pad, n_rows, tm=2 * TOK_TM, row_tm=MOE_TM):
    n = h2.shape[0] // TOK_ROWS
    n_s = h2_s.shape[0] // TOK_ROWS
    nt = n // tm
    return pl.pallas_call(
        functools.partial(_dispatch_kernel, tm=tm, n_s=n_s, row_tm=row_tm, n_tiles=n_rows // row_tm),
        grid_spec=pltpu.PrefetchScalarGridSpec(
            num_scalar_prefetch=3,
            grid=(nt,),
            in_specs=[pl.BlockSpec((None, 1, tm * TOP_K), lambda i, *_: (i, 0, 0), memory_space=pltpu.SMEM),
                      pl.BlockSpec((tm * TOK_ROWS, LANES), lambda i, *_: (i, 0)),
                      pl.BlockSpec((1, n_s * TOP_K), lambda i, *_: (0, 0), memory_space=pltpu.SMEM),
                      pl.BlockSpec((n_s * TOK_ROWS, LANES), lambda i, *_: (0, 0))],
            out_specs=pl.BlockSpec(memory_space=pl.ANY),
            scratch_shapes=[pltpu.VMEM((row_tm * TOK_ROWS, LANES), F32), pltpu.SemaphoreType.DMA(())],
        ),
        out_shape=jax.ShapeDtypeStruct((n_rows * TOK_ROWS, LANES), F32),
        compiler_params=_cparams(1),
        name="moe_dispatch",
    )(*pad, dest.reshape(nt, 1, tm * TOP_K), h2, dest_s.reshape(1, n_s * TOP_K), h2_s)


def _zero_fill(ps_ref, pn_ref, t0_ref, xs_ref, zbuf, sem, tm, n_tiles):
    zbuf[...] = jnp.zeros_like(zbuf)
    bits = [tm >> (k + 1) for k in range(tm.bit_length() - 1)]

    def segments(act):
        def seg(e, c):
            off = ps_ref[e]
            ln = pn_ref[e]
            for bit in bits:
                @pl.when((ln & bit) != 0)
                def _(off=off, bit=bit):
                    act(pltpu.make_async_copy(_tok(zbuf, 0, bit), _tok(xs_ref, off, bit), sem))
                off = off + (ln & bit)
            return c
        lax.fori_loop(0, N_EXPERTS, seg, 0)

        def tile(i, c):
            act(pltpu.make_async_copy(zbuf, _tok(xs_ref, i * tm, tm), sem))
            return c
        lax.fori_loop(t0_ref[0], n_tiles, tile, 0)

    segments(lambda cp: cp.start())
    segments(lambda cp: cp.wait())


def _expert_kernel(te_ref, tv_ref, nx_ref, sl_ref, xs_ref, wg_hbm, bg_ref, wu_hbm, bu_ref,
                   wd_hbm, bd_ref, ys_ref, wbuf, wg_bf, wu_bf, wd_bf, sems, *, tm):
    i = pl.program_id(0)
    e = te_ref[i]
    slot = sl_ref[i]
    new_expert = jnp.logical_or(i == 0, e != te_ref[jnp.maximum(i - 1, 0)])

    def fetch(expert, s):
        return [pltpu.make_async_copy(w.at[expert], wbuf.at[s, k], sems.at[s, k])
                for k, w in enumerate((wg_hbm, wu_hbm, wd_hbm))]

    @pl.when(i == 0)
    def _():
        for cp in fetch(e, slot):
            cp.start()

    @pl.when(new_expert)
    def _():
        for cp in fetch(e, slot):
            cp.wait()

        @pl.when(nx_ref[i] >= 0)
        def _():
            for cp in fetch(nx_ref[i], 1 - slot):
                cp.start(priority=1)
        wg_bf[...] = wbuf[slot, 0].astype(BF16)
        wu_bf[...] = wbuf[slot, 1].astype(BF16)
        wd_bf[...] = wbuf[slot, 2].astype(BF16)

    @pl.when(tv_ref[i] > 0)
    def _():
        x = _load_token_tiles(xs_ref, tm).astype(BF16)
        g = jnp.dot(x, wg_bf[...], preferred_element_type=F32) + bg_ref[...]
        u = jnp.dot(x, wu_bf[...], preferred_element_type=F32) + bu_ref[...]
        g = jnp.minimum(g, SWIGLU_LIMIT)
        u = jnp.clip(u, -SWIGLU_LIMIT, SWIGLU_LIMIT)
        act = (u + 1.0) * g * (1.0 / (1.0 + jnp.exp(-SWIGLU_ALPHA * g)))
        y = jnp.dot(act.astype(BF16), wd_bf[...], preferred_element_type=F32) + bd_ref[...]
        _store_token_tiles(ys_ref, y)

    @pl.when(tv_ref[i] == 0)
    def _():
        ys_ref[...] = jnp.zeros_like(ys_ref)


def _experts(xs, plan, wg, bg, wu, bu, wd, bd, tm=MOE_TM):
    n_tiles = xs.shape[0] // (tm * TOK_ROWS)
    d, f = wg.shape[-2:]
    assert d == f == D_MODEL
    b_spec = lambda n: pl.BlockSpec((None, 1, n), lambda i, te, *_: (te[i], 0, 0))
    rows = pl.BlockSpec((tm * TOK_ROWS, LANES), lambda i, *_: (i, 0))
    hbm = pl.BlockSpec(memory_space=pl.ANY)
    return pl.pallas_call(
        functools.partial(_expert_kernel, tm=tm),
        grid_spec=pltpu.PrefetchScalarGridSpec(
            num_scalar_prefetch=4,
            grid=(n_tiles,),
            in_specs=[rows, hbm, b_spec(f), hbm, b_spec(f), hbm, b_spec(d)],
            out_specs=rows,
            scratch_shapes=[pltpu.VMEM((2, 3, d, f), F32),
                            pltpu.VMEM((d, f), BF16), pltpu.VMEM((d, f), BF16), pltpu.VMEM((f, d), BF16),
                            pltpu.SemaphoreType.DMA((2, 3))],
        ),
        out_shape=jax.ShapeDtypeStruct(xs.shape, F32),
        compiler_params=_cparams(1),
        name="moe_experts",
    )(*plan, xs, wg, bg.reshape(N_EXPERTS, 1, f), wu, bu.reshape(N_EXPERTS, 1, f),
      wd, bd.reshape(N_EXPERTS, 1, d))


def _combine_kernel(dcur_ref, dnxt_ref, ys_ref, x1_ref, gate_ref, gt2_ref, gfin_ref, o_ref,
                    buf, sems, *, tm):
    i = pl.program_id(0)
    cur = i & 1
    slot_toks = TOP_K * tm

    def make(slot):
        def f(n, t, d):
            return pltpu.make_async_copy(_tok(ys_ref, d), _tok(buf, slot * slot_toks + t * tm + n),
                                         sems.at[slot])
        return f

    @pl.when(i == 0)
    def _():
        _start_rows(dcur_ref, tm, make(cur))

    @pl.when(i + 1 < pl.num_programs(0))
    def _():
        _start_rows(dnxt_ref, tm, make(1 - cur))

    _wait_rows(dcur_ref, tm, make(cur))
    base = pl.multiple_of(cur * slot_toks * TOK_ROWS, slot_toks * TOK_ROWS)
    gate = gate_ref[...]
    y = gate[:, 0:1] * _load_token_tiles(buf, tm, base=base)
    for t in range(1, TOP_K):
        y = y + gate[:, t:t + 1] * _load_token_tiles(buf, tm, base=base + t * tm * TOK_ROWS)
    x2 = x1_ref[...] + gt2_ref[...] * y
    o_ref[...] = _rmsnorm(x2, gfin_ref[...])


def _combine(ys, dest, x1, gate, gt2, gfin, seq, tm=TOK_TM):
    d = D_MODEL
    n = x1.shape[0]
    tm = min(tm, n)
    nt = n // tm
    row = lambda w: pl.BlockSpec((tm, w), lambda i: (i, 0))
    if seq:
        mod = pl.BlockSpec((None, 1, d), lambda i: (i // (seq // tm), 0, 0))
    else:
        mod = row(d)
    dspec = lambda fn: pl.BlockSpec((None, 1, tm * TOP_K), lambda i: (fn(i), 0, 0), memory_space=pltpu.SMEM)
    dest3 = dest.reshape(nt, 1, tm * TOP_K)
    return pl.pallas_call(
        functools.partial(_combine_kernel, tm=tm),
        grid=(nt,),
        in_specs=[dspec(lambda i: i), dspec(lambda i: jnp.minimum(i + 1, nt - 1)),
                  pl.BlockSpec(memory_space=pl.ANY), row(d), row(LANES), mod,
                  pl.BlockSpec((1, d), lambda i: (0, 0))],
        out_specs=row(d),
        out_shape=jax.ShapeDtypeStruct(x1.shape, F32),
        scratch_shapes=[pltpu.VMEM((2 * TOP_K * tm * TOK_ROWS, LANES), F32),
                        pltpu.SemaphoreType.DMA((2,))],
        compiler_params=_cparams(1),
        name="moe_combine",
    )(dest3, dest3, ys, x1, gate, gt2, gfin)


def _routing_plan(counts, n_pairs, tm=MOE_TM):
    pc = (counts + tm - 1) // tm * tm
    pend = jnp.cumsum(pc)
    pstart = pend - pc
    n_rows = -(-(n_pairs + N_EXPERTS * (tm - 1)) // tm) * tm
    n_tiles = n_rows // tm
    tile_row = jnp.arange(n_tiles, dtype=I32) * tm
    last_used = jnp.max(jnp.where(pc > 0, jnp.arange(N_EXPERTS, dtype=I32), 0))
    tile_e = jnp.minimum(jnp.sum((tile_row[:, None] >= pend[None, :]).astype(I32), axis=1), last_used)
    tile_valid = (tile_row < pend[-1]).astype(I32)
    ids = jnp.arange(N_EXPERTS, dtype=I32)
    used = pc > 0
    slot_e = (jnp.cumsum(used.astype(I32)) - 1) & 1
    later = jnp.where(used[None, :] & (ids[None, :] > ids[:, None]), ids[None, :], N_EXPERTS)
    next_e = jnp.min(later, axis=1)
    next_e = jnp.where(next_e == N_EXPERTS, -1, next_e)
    pick = (tile_e[:, None] == ids[None, :]).astype(I32)
    plan = (tile_e, tile_valid, jnp.sum(pick * next_e[None, :], axis=1),
            jnp.sum(pick * slot_e[None, :], axis=1))
    pad = ((pstart + counts).astype(I32), (pc - counts).astype(I32),
           (pend[-1:] // tm).astype(I32))
    return pstart.astype(I32), plan, n_rows, pad


def _rope_tables(pos):
    half = HEAD_DIM // 2
    inv = ROPE_THETA ** (-np.arange(half, dtype=np.float64) / half)
    ang = np.asarray(pos, np.float64)[:, None] * inv[None, :]
    cos = np.cos(ang)
    sin = np.sin(ang)
    cosf = np.concatenate([cos, cos, cos, cos], axis=1).astype(np.float32)
    sinf = np.concatenate([-sin, sin, -sin, sin], axis=1).astype(np.float32)
    return jnp.asarray(cosf), jnp.asarray(sinf)


def kernel(x_prompt, x_sample, cache_k, cache_v, state_conv, c_prompt, c_sample, w_ada, b_ada,
           g_norm_mix, w_in, conv_w, g_attn_out, g_conv_out, w_out, g_norm_ffn, w_router, b_router,
           w_gate, b_gate, w_up, b_up, w_down, b_down, g_final):
    depth = w_in.shape[0]
    assert depth == 1, "single-layer trunk"
    bp, sp, d = x_prompt.shape
    bs, ts, _ = x_sample.shape
    ns = bs * ts
    n_cache = cache_k.shape[2]
    l = 0

    mods = _modulations(jnp.concatenate([c_prompt, c_sample], axis=0), w_ada[l], b_ada[l])
    mp = [m.reshape(bp, 1, d) for m in jnp.split(mods[:bp], 6, axis=-1)]
    ms = [jnp.repeat(m, ts, axis=0) for m in jnp.split(mods[bp:], 6, axis=-1)]

    w_in_bf = w_in[l].astype(BF16)
    w_out_bf = w_out[l].astype(BF16)
    wr_pad = jnp.pad(w_router[l], ((0, 0), (0, LANES - N_EXPERTS)))
    br_pad = jnp.pad(b_router[l].reshape(1, N_EXPERTS), ((0, 0), (0, LANES - N_EXPERTS)),
                     constant_values=NEG)
    g_mix = g_norm_mix[l].reshape(1, d)
    g_ffn = g_norm_ffn[l].reshape(1, d)
    g_att = g_attn_out[l].reshape(1, ATT_W)
    g_cnv = g_conv_out[l].reshape(1, CONV_W)
    g_fin = g_final.reshape(1, d)

    cos_p, sin_p = _rope_tables(np.arange(sp))
    q_p, k_p, v_p, conv_p, tail_p = _inproj_prompt(x_prompt, mp[0], mp[1], g_mix, w_in_bf,
                                                   cos_p, sin_p, conv_w[l])
    att_p = _attn_prompt(q_p, k_p, v_p)

    xs_rows = x_sample.reshape(ns, d)
    cos_s, sin_s = _rope_tables(np.tile(PAST_LEN + np.arange(ts), bs))
    st = state_conv[l]
    zrow = jnp.zeros((bs, 1, CONV_W), F32)
    s1 = jnp.concatenate([st[:, 1:2], zrow, zrow, zrow], axis=1).reshape(ns, CONV_W)
    s2 = jnp.concatenate([st[:, 0:1], st[:, 1:2], zrow, zrow], axis=1).reshape(ns, CONV_W)
    q_s, k_s, v_s, conv_s, cu_s = _inproj_sample(xs_rows, ms[0], ms[1], g_mix, w_in_bf,
                                                 cos_s, sin_s, conv_w[l], s1, s2, ts)
    rows8 = lambda a: jnp.pad(a.reshape(bs, ts, ATT_W), ((0, 0), (0, 8 - ts), (0, 0)))
    kt = jnp.transpose(cache_k[l], (0, 2, 3, 1))
    vt = jnp.transpose(cache_v[l], (0, 2, 3, 1))
    att_s = _attn_sample(rows8(q_s), kt, vt, rows8(k_s), rows8(v_s), ts)[:, :ts].reshape(ns, ATT_W)

    cnt0 = jnp.zeros((8, LANES), F32)
    x1_p, h2_p, idx_p, gate_p, rank_p, cnt_p = _merge(
        att_p, conv_p, x_prompt, mp[2], mp[3], mp[4], g_att, g_cnv, g_ffn,
        w_out_bf, wr_pad, br_pad, cnt0, 2 * TOK_TM)
    x1_s, h2_s, idx_s, gate_s, rank_s, cnt_all = _merge(
        att_s, conv_s, xs_rows, ms[2], ms[3], ms[4], g_att, g_cnv, g_ffn,
        w_out_bf, wr_pad, br_pad, cnt_p, ns)

    n_tok = bp * sp + ns
    counts = cnt_all[0, :N_EXPERTS].astype(I32)
    pstart, plan, n_rows, pad = _routing_plan(counts, n_tok * TOP_K)
    ids = jnp.arange(N_EXPERTS, dtype=I32)
    slot_of = lambda idx, rank: jnp.sum(
        jnp.where(idx[..., :TOP_K, None] == ids, pstart, 0), axis=-1) + rank[..., :TOP_K]
    dest_p = slot_of(idx_p, rank_p).reshape(bp * sp, TOP_K)
    dest_s = slot_of(idx_s, rank_s)
    xs_sorted = _dispatch(h2_p.reshape(bp * sp * TOK_ROWS, LANES), dest_p, h2_s, dest_s, pad, n_rows)
    ys = _experts(xs_sorted, plan, w_gate[l], b_gate[l], w_up[l], b_up[l], w_down[l], b_down[l])
    y_prompt = _combine(ys, dest_p, x1_p.reshape(bp * sp, d), gate_p.reshape(bp * sp, LANES),
                        mp[5], g_fin, sp).reshape(bp, sp, d)
    y_sample = _combine(ys, dest_s, x1_s, gate_s, ms[5], g_fin, 0).reshape(bs, ts, d)

    heads = lambda a, b, s: a.reshape(1, b, s, N_HEADS, HEAD_DIM)
    keep = min(WINDOW_MAX, sp)
    return (y_prompt, y_sample,
            heads(k_p, bp, sp)[:, :, sp - keep:], heads(v_p, bp, sp)[:, :, sp - keep:],
            tail_p[:, 8 - (CONV_K - 1):][None],
            heads(k_s, bs, ts), heads(v_s, bs, ts),
            cu_s.reshape(bs, ts, CONV_W)[:, ts - (CONV_K - 1):][None])
```
